```python
import jax, jax.numpy as jnp
from jax import lax
import numpy as np

D_MODEL = 1024
BATCH = 16
SEQ = 4096
DEPTH = 1

CTX_LEN = 256
GRID_W = 64
RET_W = D_MODEL // 2
RET_HEADS = 4
RET_HEAD_DIM = RET_W // RET_HEADS
FOURIER_W = D_MODEL - RET_W
FOURIER_GROUPS = 4
FOURIER_GROUP_DIM = FOURIER_W // FOURIER_GROUPS
MIX_W = RET_W + FOURIER_W
IN_W = 4 * RET_W + FOURIER_W
D_FF = ((8 * D_MODEL // 3 + 127) // 128) * 128
CONV_WIDTH = 3
CHUNK = 128
ROPE_BASE = 10000.0
NORM_EPS = 1e-6
N_MOD = 6

kernel_name = "hybrid_retention_fourier_convffn_dit"


def rms_norm(x, g):
    xf = x.astype(jnp.float32)
    xf = xf * lax.rsqrt(jnp.mean(xf * xf, axis=-1, keepdims=True) + NORM_EPS)
    return xf.astype(x.dtype) * g


def modulate(h, shift, scale):
    return h * (1 + scale) + shift


def adaln(cond, w, b):
    return jnp.split(jax.nn.silu(cond) @ w + b, N_MOD, axis=-1)


def split_heads(t):
    bsz, length, _ = t.shape
    return t.reshape(bsz, length, RET_HEADS, RET_HEAD_DIM).transpose(0, 2, 1, 3).astype(jnp.float32)


def axial_rope(length):
    t = jnp.arange(length)
    row = (t // GRID_W).astype(jnp.float32)
    col = (t % GRID_W).astype(jnp.float32)
    n_freq = RET_HEAD_DIM // 4
    freqs = ROPE_BASE ** (-jnp.arange(n_freq, dtype=jnp.float32) / n_freq)
    ang = jnp.concatenate([row[:, None] * freqs, col[:, None] * freqs], axis=-1)
    return jnp.cos(ang), jnp.sin(ang)


def apply_rope(t, cos, sin):
    half = RET_HEAD_DIM // 2
    t1, t2 = t[..., :half], t[..., half:]
    return jnp.concatenate([t1 * cos - t2 * sin, t1 * sin + t2 * cos], axis=-1)


def retention_scan(q, k, v, log_gamma, init_state, strict):
    bsz, heads, length, dk = q.shape
    dv = v.shape[-1]
    n_chunks = length // CHUNK
    pos = jnp.arange(CHUNK, dtype=jnp.float32)
    diff = pos[:, None] - pos[None, :]
    mask = diff > 0 if strict else diff >= 0
    dmask = jnp.where(mask[None], jnp.exp(jnp.where(mask, diff, 0.0)[None] * log_gamma[:, None, None]), 0.0)
    qc = q.reshape(bsz, heads, n_chunks, CHUNK, dk)
    kc = k.reshape(bsz, heads, n_chunks, CHUNK, dk)
    vc = v.reshape(bsz, heads, n_chunks, CHUNK, dv)
    scores = jnp.einsum('bhncd,bhnmd->bhncm', qc, kc) * dmask[:, None]
    intra = jnp.einsum('bhncm,bhnme->bhnce', scores, vc)
    zeta = jnp.exp((CHUNK - 1 - pos)[None, :] * log_gamma[:, None])
    xi = jnp.exp((pos + 1)[None, :] * log_gamma[:, None])
    kv = jnp.einsum('bhnmd,bhnme->nbhde', kc * zeta[None, :, None, :, None], vc)
    chunk_decay = jnp.exp(CHUNK * log_gamma)[None, :, None, None]

    def step(state, kv_i):
        return state * chunk_decay + kv_i, state

    _, prev = lax.scan(step, init_state, kv)
    cross = jnp.einsum('bhncd,nbhde->bhnce', qc * xi[None, :, None, :, None], prev)
    return (intra + cross).reshape(bsz, heads, length, dv)


def context_states(h_ctx, w_in, lg_f, lg_b):
    kv = h_ctx @ w_in[:, RET_W:3 * RET_W]
    k, v = jnp.split(kv, 2, axis=-1)
    k = split_heads(k) * RET_HEAD_DIM ** -0.5
    v = split_heads(v)
    length = k.shape[2]
    m = jnp.arange(length, dtype=jnp.float32)
    w_f = jnp.exp((length - 1 - m)[None, :] * lg_f[:, None])
    w_b = jnp.exp(m[None, :] * lg_b[:, None])
    s_f = jnp.einsum('bhld,hl,bhle->bhde', k, w_f, v)
    s_b = jnp.einsum('bhld,hl,bhle->bhde', k, w_b, v)
    return s_f, s_b


def fourier_mix(f):
    bsz, length, _ = f.shape
    fg = f.reshape(bsz, length, FOURIER_GROUPS, FOURIER_GROUP_DIM).astype(jnp.float32)
    out = jnp.fft.fft2(fg, axes=(1, 3), norm="ortho").real
    return out.reshape(bsz, length, FOURIER_W).astype(f.dtype)


def token_mixer(h, w_in, w_out, lg_f, lg_b, init_f, init_b, rope):
    proj = h @ w_in
    q, k, v, g, f = jnp.split(proj, [RET_W, 2 * RET_W, 3 * RET_W, 4 * RET_W], axis=-1)
    q = split_heads(q)
    k = split_heads(k) * RET_HEAD_DIM ** -0.5
    v = split_heads(v)
    if rope is not None:
        q = apply_rope(q, *rope)
        k = apply_rope(k, *rope)
    o_f = retention_scan(q, k, v, lg_f, init_f, strict=False)
    flip = lambda t: jnp.flip(t, axis=2)
    o_b = flip(retention_scan(flip(q), flip(k), flip(v), lg_b, init_b, strict=True))
    o = o_f + o_b
    o = o * lax.rsqrt(jnp.mean(o * o, axis=-1, keepdims=True) + NORM_EPS)
    bsz, length = h.shape[0], h.shape[1]
    o = o.transpose(0, 2, 1, 3).reshape(bsz, length, RET_W).astype(h.dtype)
    ret = o * jax.nn.silu(g)
    return jnp.concatenate([ret, fourier_mix(f)], axis=-1) @ w_out


def conv_ffn(h, w_up, conv_w, conv_b, w_down, rows):
    bsz, length, _ = h.shape
    u = (h @ w_up).reshape(bsz, rows, length // rows, 2 * D_FF)
    up = jnp.pad(u, ((0, 0), (0, 0), (1, 1), (0, 0)))
    u = up[:, :, :-2] * conv_w[0] + up[:, :, 1:-1] * conv_w[1] + up[:, :, 2:] * conv_w[2] + conv_b
    a, b = jnp.split(u.reshape(bsz, length, 2 * D_FF), 2, axis=-1)
    return (jax.nn.silu(a) * b) @ w_down


def setup_inputs(seed: int = 0) -> dict:
    key = jax.random.key(seed)
    ks = jax.random.split(key, 20)
    nrm = lambda k, shape, s: jax.random.normal(k, shape, jnp.float32) * s
    base = 1.0 - 2.0 ** (-5.0 - jnp.arange(RET_HEADS, dtype=jnp.float32))
    logit = jnp.log(base / (1.0 - base))
    return {
        "x": nrm(ks[0], (BATCH, SEQ, D_MODEL), 1.0),
        "c": nrm(ks[1], (BATCH, D_MODEL), 1.0),
        "ctx": nrm(ks[2], (BATCH, CTX_LEN, D_MODEL), 1.0),
        "c_ctx": nrm(ks[3], (D_MODEL,), 1.0),
        "w_ada": nrm(ks[4], (DEPTH, D_MODEL, N_MOD * D_MODEL), D_MODEL ** -0.5),
        "b_ada": nrm(ks[5], (DEPTH, N_MOD * D_MODEL), 0.02),
        "g_mix_pre": 1.0 + nrm(ks[6], (DEPTH, D_MODEL), 0.05),
        "g_mix_post": 1.0 + nrm(ks[7], (DEPTH, D_MODEL), 0.05),
        "g_ffn_pre": 1.0 + nrm(ks[8], (DEPTH, D_MODEL), 0.05),
        "g_ffn_post": 1.0 + nrm(ks[9], (DEPTH, D_MODEL), 0.05),
        "w_in": nrm(ks[10], (DEPTH, D_MODEL, IN_W), D_MODEL ** -0.5),
        "ret_decay_fwd": logit[None] + nrm(ks[11], (DEPTH, RET_HEADS), 0.1),
        "ret_decay_bwd": logit[None] + nrm(ks[12], (DEPTH, RET_HEADS), 0.1),
        "w_out": nrm(ks[13], (DEPTH, MIX_W, D_MODEL), MIX_W ** -0.5),
        "w_up": nrm(ks[14], (DEPTH, D_MODEL, 2 * D_FF), D_MODEL ** -0.5),
        "conv_w": nrm(ks[15], (DEPTH, CONV_WIDTH, 2 * D_FF), CONV_WIDTH ** -0.5),
        "conv_b": nrm(ks[16], (DEPTH, 2 * D_FF), 0.02),
        "w_down": nrm(ks[17], (DEPTH, D_FF, D_MODEL), D_FF ** -0.5),
    }


def reference(x, c, ctx, c_ctx, w_ada, b_ada, g_mix_pre, g_mix_post, g_ffn_pre, g_ffn_post,
              w_in, ret_decay_fwd, ret_decay_bwd, w_out, w_up, conv_w, conv_b, w_down):
    seq_len = x.shape[1]
    rows = seq_len // GRID_W
    rope = axial_rope(seq_len)
    ctx_s = ctx
    for layer in range(DEPTH):
        lg_f = jax.nn.log_sigmoid(ret_decay_fwd[layer].astype(jnp.float32))
        lg_b = jax.nn.log_sigmoid(ret_decay_bwd[layer].astype(jnp.float32))
        sh_a, sc_a, gt_a, sh_f, sc_f, gt_f = [m[:, None, :] for m in adaln(c, w_ada[layer], b_ada[layer])]
        csh_a, csc_a, cgt_a, csh_f, csc_f, cgt_f = adaln(c_ctx, w_ada[layer], b_ada[layer])
        h_ctx = modulate(rms_norm(ctx_s, g_mix_pre[layer]), csh_a, csc_a)
        s_f, s_b = context_states(h_ctx, w_in[layer], lg_f, lg_b)
        h_lat = modulate(rms_norm(x, g_mix_pre[layer]), sh_a, sc_a)
        mix = token_mixer(h_lat, w_in[layer], w_out[layer], lg_f, lg_b, s_f, s_b, rope)
        x = x + gt_a * rms_norm(mix, g_mix_post[layer])
        h_ffn = modulate(rms_norm(x, g_ffn_pre[layer]), sh_f, sc_f)
        ffn = conv_ffn(h_ffn, w_up[layer], conv_w[layer], conv_b[layer], w_down[layer], rows)
        x = x + gt_f * rms_norm(ffn, g_ffn_post[layer])
        if layer + 1 < DEPTH:
            zero = jnp.zeros_like(s_f)
            mix_c = token_mixer(h_ctx, w_in[layer], w_out[layer], lg_f, lg_b, zero, zero, None)
            ctx_s = ctx_s + cgt_a * rms_norm(mix_c, g_mix_post[layer])
            h_cffn = modulate(rms_norm(ctx_s, g_ffn_pre[layer]), csh_f, csc_f)
            ffn_c = conv_ffn(h_cffn, w_up[layer], conv_w[layer], conv_b[layer], w_down[layer], 1)
            ctx_s = ctx_s + cgt_f * rms_norm(ffn_c, g_ffn_post[layer])
    return x
```

```python
import functools
import math

import numpy as np
import jax
import jax.numpy as jnp
from jax import lax
from jax.experimental import pallas as pl
from jax.experimental.pallas import tpu as pltpu

F32 = jnp.float32
BF16 = jnp.bfloat16

RET_HEADS = 4
HEAD_DIM = 128
RET_W = RET_HEADS * HEAD_DIM
FOURIER_GROUPS = 4
GROUP_DIM = 128
FOURIER_W = FOURIER_GROUPS * GROUP_DIM
GRID_W = 64
CHUNK = 128
ROPE_BASE = 10000.0
NORM_EPS = 1e-6
N_MOD = 6
CONV_WIDTH = 3
COND_ROWS = 24
FF_CHUNK = 256
VMEM_LIMIT = 56 * 1024 * 1024


def _silu(x):
    return x / (1.0 + jnp.exp(-x))


def _rms_norm(x, g):
    ms = jnp.mean(x * x, axis=-1, keepdims=True)
    return x * lax.rsqrt(ms + NORM_EPS) * g


def _log_sigmoid(x):
    return jnp.minimum(x, 0.0) - jnp.log(1.0 + jnp.exp(-jnp.abs(x)))


def _params(*sem):
    return pltpu.CompilerParams(dimension_semantics=sem, vmem_limit_bytes=VMEM_LIMIT)


def _adaln_kernel(cond_ref, w_ref, b_ref, o_ref):
    s = _silu(cond_ref[...])
    o_ref[...] = jnp.dot(s, w_ref[...], preferred_element_type=F32) + b_ref[...]


def _adaln(cond, w, b):
    rows, d = cond.shape
    n = w.shape[1]
    tn = d
    return pl.pallas_call(
        _adaln_kernel,
        out_shape=jax.ShapeDtypeStruct((rows, n), F32),
        grid=(n // tn,),
        in_specs=[pl.BlockSpec((rows, d), lambda j: (0, 0)),
                  pl.BlockSpec((d, tn), lambda j: (0, j)),
                  pl.BlockSpec((1, tn), lambda j: (0, j))],
        out_specs=pl.BlockSpec((rows, tn), lambda j: (0, j)),
        compiler_params=_params("arbitrary"),
        name="adaln",
    )(cond, w, b)


def _ctx_kernel(ctx_ref, sh_ref, sc_ref, g_ref, w_ref, decf_ref, decb_ref, sf_ref, sb_ref):
    n_ctx = ctx_ref.shape[1]
    h = _rms_norm(ctx_ref[0], g_ref[...]) * (1.0 + sc_ref[...]) + sh_ref[...]
    kv = jnp.dot(h.astype(BF16), w_ref[...], preferred_element_type=F32)
    lgf = _log_sigmoid(decf_ref[...])
    lgb = _log_sigmoid(decb_ref[...])
    m = lax.broadcasted_iota(jnp.int32, (n_ctx, HEAD_DIM), 0).astype(F32)
    for hd in range(RET_HEADS):
        k = kv[:, hd * HEAD_DIM:(hd + 1) * HEAD_DIM] * (HEAD_DIM ** -0.5)
        v = kv[:, RET_W + hd * HEAD_DIM:RET_W + (hd + 1) * HEAD_DIM].astype(BF16)
        w_f = jnp.exp((n_ctx - 1.0 - m) * lgf[hd:hd + 1, :])
        w_b = jnp.exp(m * lgb[hd:hd + 1, :])
        dn = (((0,), (0,)), ((), ()))
        sf_ref[0, hd] = lax.dot_general((k * w_f).astype(BF16), v, dn, preferred_element_type=F32)
        sb_ref[0, hd] = lax.dot_general((k * w_b).astype(BF16), v, dn, preferred_element_type=F32)


def _ctx_states(ctx, sh, sc, g, w_kv, decf, decb):
    bsz, n_ctx, d = ctx.shape
    st = jax.ShapeDtypeStruct((bsz, RET_HEADS, HEAD_DIM, HEAD_DIM), F32)
    st_spec = pl.BlockSpec((1, RET_HEADS, HEAD_DIM, HEAD_DIM), lambda b: (b, 0, 0, 0))
    vec = pl.BlockSpec((1, d), lambda b: (0, 0))
    dec = pl.BlockSpec((RET_HEADS, HEAD_DIM), lambda b: (0, 0))
    return pl.pallas_call(
        _ctx_kernel,
        out_shape=(st, st),
        grid=(bsz,),
        in_specs=[pl.BlockSpec((1, n_ctx, d), lambda b: (b, 0, 0)), vec, vec, vec,
                  pl.BlockSpec(w_kv.shape, lambda b: (0, 0)), dec, dec],
        out_specs=(st_spec, st_spec),
        compiler_params=_params("arbitrary"),
        name="ctx_state",
    )(ctx, sh, sc, g, w_kv, decf, decb)


def _in_proj_kernel(x_ref, sh_ref, sc_ref, g_ref, w_ref, cos_ref, sin_ref,
                    q_ref, k_ref, v_ref, gate_ref, f_ref):
    h = _rms_norm(x_ref[0], g_ref[...]) * (1.0 + sc_ref[0]) + sh_ref[0]
    proj = jnp.dot(h.astype(BF16), w_ref[...], preferred_element_type=F32)
    cosf = cos_ref[...]
    sinf = sin_ref[...]
    for hd in range(RET_HEADS):
        lo = hd * HEAD_DIM
        q = proj[:, lo:lo + HEAD_DIM]
        k = proj[:, RET_W + lo:RET_W + lo + HEAD_DIM] * (HEAD_DIM ** -0.5)
        q_ref[0, :, lo:lo + HEAD_DIM] = (q * cosf + pltpu.roll(q, HEAD_DIM // 2, 1) * sinf).astype(BF16)
        k_ref[0, :, lo:lo + HEAD_DIM] = (k * cosf + pltpu.roll(k, HEAD_DIM // 2, 1) * sinf).astype(BF16)
    v_ref[0] = proj[:, 2 * RET_W:3 * RET_W].astype(BF16)
    gate_ref[0] = proj[:, 3 * RET_W:4 * RET_W].astype(BF16)
    f_ref[0] = proj[:, 4 * RET_W:].astype(BF16)


def _in_proj(x, sh, sc, g, w_in, cosf, sinf, tm):
    bsz, seq, d = x.shape
    mod = pl.BlockSpec((1, 1, d), lambda b, i: (b, 0, 0))
    tok = lambda w: pl.BlockSpec((1, tm, w), lambda b, i: (b, i, 0))
    out = lambda w: jax.ShapeDtypeStruct((bsz, seq, w), BF16)
    rope = pl.BlockSpec((tm, HEAD_DIM), lambda b, i: (i, 0))
    return pl.pallas_call(
        _in_proj_kernel,
        out_shape=(out(RET_W), out(RET_W), out(RET_W), out(RET_W), out(FOURIER_W)),
        grid=(bsz, seq // tm),
        in_specs=[tok(d), mod, mod, pl.BlockSpec((1, d), lambda b, i: (0, 0)),
                  pl.BlockSpec(w_in.shape, lambda b, i: (0, 0)), rope, rope],
        out_specs=(tok(RET_W), tok(RET_W), tok(RET_W), tok(RET_W), tok(FOURIER_W)),
        compiler_params=_params("arbitrary", "arbitrary"),
        name="in_proj",
    )(x, sh, sc, g, w_in, cosf, sinf)


def _retention_kernel(q_ref, k_ref, v_ref, gate_ref, sf0_ref, sb0_ref, decf_ref, decb_ref,
                      o_ref, kvf_scr, kvb_scr, sf_scr, sb_scr):
    hd = pl.program_id(1)
    n_chunks = q_ref.shape[1] // CHUNK
    lgf = _log_sigmoid(decf_ref[pl.ds(hd, 1), :])
    lgb = _log_sigmoid(decb_ref[pl.ds(hd, 1), :])
    pos = lax.broadcasted_iota(jnp.int32, (CHUNK, HEAD_DIM), 0).astype(F32)
    col = lax.broadcasted_iota(jnp.int32, (CHUNK, CHUNK), 1).astype(F32)
    zeta_f = jnp.exp((CHUNK - 1.0 - pos) * lgf)
    zeta_b = jnp.exp(pos * lgb)
    xi_f = jnp.exp((pos + 1.0) * lgf)
    xi_b = jnp.exp((CHUNK - pos) * lgb)
    diff = pos - col
    decay = jnp.exp(jnp.where(diff >= 0, diff * lgf, -diff * lgb))
    cdec_f = jnp.exp(CHUNK * lgf)
    cdec_b = jnp.exp(CHUNK * lgb)
    tn = (((0,), (0,)), ((), ()))
    nt = (((1,), (1,)), ((), ()))

    def chunk_kv(n, carry):
        rows = pl.ds(pl.multiple_of(n * CHUNK, CHUNK), CHUNK)
        k = k_ref[0, rows, :].astype(F32)
        v = v_ref[0, rows, :]
        kvf_scr[n] = lax.dot_general((k * zeta_f).astype(BF16), v, tn, preferred_element_type=F32)
        kvb_scr[n] = lax.dot_general((k * zeta_b).astype(BF16), v, tn, preferred_element_type=F32)
        return carry

    lax.fori_loop(0, n_chunks, chunk_kv, 0)

    def scan_f(n, state):
        sf_scr[n] = state.astype(BF16)
        return state * cdec_f + kvf_scr[n]

    lax.fori_loop(0, n_chunks, scan_f, sf0_ref[0, 0])

    def scan_b(i, state):
        n = n_chunks - 1 - i
        sb_scr[n] = state.astype(BF16)
        return state * cdec_b + kvb_scr[n]

    lax.fori_loop(0, n_chunks, scan_b, sb0_ref[0, 0])

    def chunk_out(n, carry):
        rows = pl.ds(pl.multiple_of(n * CHUNK, CHUNK), CHUNK)
        q = q_ref[0, rows, :]
        k = k_ref[0, rows, :]
        v = v_ref[0, rows, :]
        scores = lax.dot_general(q, k, nt, preferred_element_type=F32) * decay
        qf = q.astype(F32)
        o = jnp.dot(scores.astype(BF16), v, preferred_element_type=F32)
        o = o + jnp.dot((qf * xi_f).astype(BF16), sf_scr[n], preferred_element_type=F32)
        o = o + jnp.dot((qf * xi_b).astype(BF16), sb_scr[n], preferred_element_type=F32)
        o = o * lax.rsqrt(jnp.mean(o * o, axis=-1, keepdims=True) + NORM_EPS)
        o_ref[0, rows, :] = (o * _silu(gate_ref[0, rows, :].astype(F32))).astype(BF16)
        return carry

    lax.fori_loop(0, n_chunks, chunk_out, 0)


def _retention(q, k, v, gate, sf0, sb0, decf, decb):
    bsz, seq, _ = q.shape
    n_chunks = seq // CHUNK
    tok = pl.BlockSpec((1, seq, HEAD_DIM), lambda b, h: (b, 0, h))
    st = pl.BlockSpec((1, 1, HEAD_DIM, HEAD_DIM), lambda b, h: (b, h, 0, 0))
    dec = pl.BlockSpec((RET_HEADS, HEAD_DIM), lambda b, h: (0, 0))
    return pl.pallas_call(
        _retention_kernel,
        out_shape=jax.ShapeDtypeStruct((bsz, seq, RET_W), BF16),
        grid=(bsz, RET_HEADS),
        in_specs=[tok, tok, tok, tok, st, st, dec, dec],
        out_specs=tok,
        scratch_shapes=[pltpu.VMEM((n_chunks, HEAD_DIM, HEAD_DIM), F32),
                        pltpu.VMEM((n_chunks, HEAD_DIM, HEAD_DIM), F32),
                        pltpu.VMEM((n_chunks, HEAD_DIM, HEAD_DIM), BF16),
                        pltpu.VMEM((n_chunks, HEAD_DIM, HEAD_DIM), BF16)],
        compiler_params=_params("arbitrary", "arbitrary"),
        name="retention",
    )(q, k, v, gate, sf0, sb0, decf, decb)


def _fft_rows_kernel(w_ref, f_ref, a_ref):
    res = jnp.dot(w_ref[...], f_ref[0], preferred_element_type=F32)
    gh = res.shape[0] // 2
    a_ref[0, 0] = res[:gh].astype(BF16)
    a_ref[0, 1] = res[gh:].astype(BF16)


def _fft_rows(f2, w1, nb):
    bsz, gh, width = f2.shape
    return pl.pallas_call(
        _fft_rows_kernel,
        out_shape=jax.ShapeDtypeStruct((bsz, 2, gh, width), BF16),
        grid=(bsz, width // nb),
        in_specs=[pl.BlockSpec(w1.shape, lambda b, j: (0, 0)),
                  pl.BlockSpec((1, gh, nb), lambda b, j: (b, 0, j))],
        out_specs=pl.BlockSpec((1, 2, gh, nb), lambda b, j: (b, 0, 0, j)),
        compiler_params=_params("arbitrary", "arbitrary"),
        name="fft_rows",
    )(w1, f2)


def _fft_cols_kernel(a_ref, twc_ref, tws_ref, w2_ref, wc_ref, o_ref):
    kb = a_ref.shape[2] // GRID_W
    ar = a_ref[0, 0].astype(F32)
    ai = a_ref[0, 1].astype(F32)
    twc = jnp.concatenate([twc_ref[...]] * FOURIER_GROUPS, axis=1)
    tws = jnp.concatenate([tws_ref[...]] * FOURIER_GROUPS, axis=1)
    br = (ar * twc + ai * tws).astype(BF16)
    bi = (ai * twc - ar * tws).astype(BF16)
    xr, xi = [], []
    for j in range(kb):
        rows = slice(j * GRID_W, (j + 1) * GRID_W)
        stacked = jnp.concatenate([br[rows], bi[rows]], axis=0)
        res = jnp.dot(w2_ref[...], stacked, preferred_element_type=F32)
        xr.append(res[:GRID_W])
        xi.append(res[GRID_W:])
    xr = jnp.concatenate(xr, axis=0).astype(BF16)
    xi = jnp.concatenate(xi, axis=0).astype(BF16)
    for g in range(FOURIER_GROUPS):
        lanes = slice(g * GROUP_DIM, (g + 1) * GROUP_DIM)
        lhs = jnp.concatenate([xr[:, lanes], xi[:, lanes]], axis=1)
        res = jnp.dot(lhs, wc_ref[...], preferred_element_type=F32).astype(BF16)
        for j in range(kb):
            lo = j * FOURIER_W + g * GROUP_DIM
            o_ref[0, :, lo:lo + GROUP_DIM] = res[j * GRID_W:(j + 1) * GRID_W]


def _fft_cols(a3, twc, tws, w2, wc, kb):
    bsz, _, seq, width = a3.shape
    gh = seq // GRID_W
    tw = pl.BlockSpec((kb * GRID_W, GROUP_DIM), lambda b, j: (j, 0))
    return pl.pallas_call(
        _fft_cols_kernel,
        out_shape=jax.ShapeDtypeStruct((bsz, GRID_W, gh * width), BF16),
        grid=(bsz, gh // kb),
        in_specs=[pl.BlockSpec((1, 2, kb * GRID_W, width), lambda b, j: (b, 0, j, 0)), tw, tw,
                  pl.BlockSpec(w2.shape, lambda b, j: (0, 0)),
                  pl.BlockSpec(wc.shape, lambda b, j: (0, 0))],
        out_specs=pl.BlockSpec((1, GRID_W, kb * width), lambda b, j: (b, 0, j)),
        compiler_params=_params("arbitrary", "arbitrary"),
        name="fft_cols",
    )(a3, twc, tws, w2, wc)


def _dft_constants(seq):
    gh = seq // GRID_W
    def cs(n_rows, n_cols, period):
        ang = 2.0 * np.pi * np.outer(np.arange(n_rows), np.arange(n_cols)) / period
        return np.cos(ang), np.sin(ang)
    c1, s1 = cs(gh, gh, gh)
    w1 = np.concatenate([c1, -s1], axis=0) / gh
    c2, s2 = cs(GRID_W, GRID_W, GRID_W)
    w2 = np.block([[c2, s2], [-s2, c2]]) / math.sqrt(GRID_W * gh) * gh
    cc, sc = cs(GROUP_DIM, GROUP_DIM, GROUP_DIM)
    wc = np.concatenate([cc, sc], axis=0) / math.sqrt(GROUP_DIM)
    k1 = np.repeat(np.arange(gh), GRID_W)
    t2 = np.tile(np.arange(GRID_W), gh)
    ang = 2.0 * np.pi * (k1 * t2 % seq) / seq
    twc = np.broadcast_to(np.cos(ang)[:, None], (seq, GROUP_DIM))
    tws = np.broadcast_to(np.sin(ang)[:, None], (seq, GROUP_DIM))
    t = np.arange(seq)
    n_freq = HEAD_DIM // 4
    freqs = ROPE_BASE ** (-np.arange(n_freq, dtype=np.float32) / n_freq)
    rang = np.concatenate([(t // GRID_W).astype(np.float32)[:, None] * freqs,
                           (t % GRID_W).astype(np.float32)[:, None] * freqs], axis=-1).astype(np.float32)
    cosf = np.concatenate([np.cos(rang), np.cos(rang)], axis=-1)
    sinf = np.concatenate([-np.sin(rang), np.sin(rang)], axis=-1)
    as_f32 = lambda a: jnp.asarray(np.ascontiguousarray(a), F32)
    as_bf16 = lambda a: jnp.asarray(np.ascontiguousarray(a), F32).astype(BF16)
    return dict(w1=as_bf16(w1), w2=as_bf16(w2), wc=as_bf16(wc), twc=as_f32(twc), tws=as_f32(tws),
                cosf=as_f32(cosf), sinf=as_f32(sinf))


def _out_ffn_kernel(x_ref, ret_ref, four_ref, mod_ref, gpost_ref, gpre_ref, gfpost_ref,
                    wo_ref, wa_ref, wb_ref, ca_ref, cb_ref, wd_ref, o_ref, acc_ref, h_ref):
    tm = x_ref.shape[1]
    n_ff = wa_ref.shape[0]
    gt_a, sh_f, sc_f, gt_f = mod_ref[0, 0:1], mod_ref[0, 1:2], mod_ref[0, 2:3], mod_ref[0, 3:4]
    mix = jnp.dot(ret_ref[0], wo_ref[:RET_W, :], preferred_element_type=F32)
    mix = mix + jnp.dot(four_ref[0], wo_ref[RET_W:, :], preferred_element_type=F32)
    x1 = x_ref[0] + gt_a * _rms_norm(mix, gpost_ref[...])
    o_ref[0] = x1
    h_ref[...] = (_rms_norm(x1, gpre_ref[...]) * (1.0 + sc_f) + sh_f).astype(BF16)
    acc_ref[...] = jnp.zeros_like(acc_ref)
    row = lax.broadcasted_iota(jnp.int32, (tm, FF_CHUNK), 0) % GRID_W
    has_prev = row != 0
    has_next = row != GRID_W - 1

    def conv(u, c_ref, j):
        prev = jnp.where(has_prev, pltpu.roll(u, 1, 0), 0.0)
        nxt = jnp.where(has_next, pltpu.roll(u, tm - 1, 0), 0.0)
        c = c_ref[j]
        return prev * c[0:1] + u * c[1:2] + nxt * c[2:3] + c[3:4]

    def ff_step(j, carry):
        h = h_ref[...]
        a = conv(jnp.dot(h, wa_ref[j], preferred_element_type=F32), ca_ref, j)
        b = conv(jnp.dot(h, wb_ref[j], preferred_element_type=F32), cb_ref, j)
        act = (_silu(a) * b).astype(BF16)
        acc_ref[...] += jnp.dot(act, wd_ref[j], preferred_element_type=F32)
        return carry

    lax.fori_loop(0, n_ff, ff_step, 0)
    o_ref[0] = o_ref[0] + gt_f * _rms_norm(acc_ref[...], gfpost_ref[...])


def _out_ffn(x, ret, four, mod, gpost, gpre, gfpost, wo, wa, wb, ca, cb, wd, tm):
    bsz, seq, d = x.shape
    tok = lambda w: pl.BlockSpec((1, tm, w), lambda b, i: (b, i, 0))
    vec = pl.BlockSpec((1, d), lambda b, i: (0, 0))
    const = lambda a: pl.BlockSpec(a.shape, lambda b, i: (0,) * a.ndim, pipeline_mode=pl.Buffered(1))
    return pl.pallas_call(
        _out_ffn_kernel,
        out_shape=jax.ShapeDtypeStruct((bsz, seq, d), F32),
        grid=(bsz, seq // tm),
        in_specs=[tok(d), tok(RET_W), tok(FOURIER_W),
                  pl.BlockSpec((1, 4, d), lambda b, i: (b, 0, 0)), vec, vec, vec,
                  const(wo), const(wa), const(wb), const(ca), const(cb), const(wd)],
        out_specs=tok(d),
        scratch_shapes=[pltpu.VMEM((tm, d), F32), pltpu.VMEM((tm, d), BF16)],
        compiler_params=_params("arbitrary", "arbitrary"),
        name="out_ffn",
    )(x, ret, four, mod, gpost, gpre, gfpost, wo, wa, wb, ca, cb, wd)


def _layer(x, ctx, cond, consts, w_ada, b_ada, g_mix_pre, g_mix_post, g_ffn_pre, g_ffn_post,
           w_in, dec_f, dec_b, w_out, w_up, conv_w, conv_b, w_down):
    bsz, seq, d = x.shape
    d_ff = w_down.shape[0]
    n_ff = d_ff // FF_CHUNK
    row = lambda v: v.reshape(1, -1)

    mod = _adaln(cond, w_ada, row(b_ada))
    mod = mod.reshape(COND_ROWS, N_MOD, d)
    lat = mod[:bsz]
    cmod = mod[bsz]

    decf = jnp.broadcast_to(dec_f[:, None], (RET_HEADS, HEAD_DIM))
    decb = jnp.broadcast_to(dec_b[:, None], (RET_HEADS, HEAD_DIM))
    w_in_bf = w_in.astype(BF16)
    sf0, sb0 = _ctx_states(ctx, cmod[0:1], cmod[1:2], row(g_mix_pre),
                           w_in_bf[:, RET_W:3 * RET_W], decf, decb)

    q, k, v, gate, f = _in_proj(x, lat[:, 0:1], lat[:, 1:2], row(g_mix_pre), w_in_bf,
                                consts["cosf"], consts["sinf"], tm=512)
    ret = _retention(q, k, v, gate, sf0, sb0, decf, decb)

    gh = seq // GRID_W
    a = _fft_rows(f.reshape(bsz, gh, GRID_W * FOURIER_W), consts["w1"], nb=4096)
    four = _fft_cols(a.reshape(bsz, 2, seq, FOURIER_W), consts["twc"], consts["tws"],
                     consts["w2"], consts["wc"], kb=8)
    four = four.reshape(bsz, seq, FOURIER_W)

    chunked = lambda w: w.reshape(w.shape[0], n_ff, FF_CHUNK).transpose(1, 0, 2)
    w_up_bf = w_up.astype(BF16)
    conv = jnp.concatenate([conv_w, conv_b[None], jnp.zeros((8 - CONV_WIDTH - 1, 2 * d_ff), F32)], axis=0)
    return _out_ffn(x, ret, four, lat[:, 2:6], row(g_mix_post), row(g_ffn_pre), row(g_ffn_post),
                    w_out.astype(BF16), chunked(w_up_bf[:, :d_ff]), chunked(w_up_bf[:, d_ff:]),
                    chunked(conv[:, :d_ff]), chunked(conv[:, d_ff:]),
                    w_down.astype(BF16).reshape(n_ff, FF_CHUNK, d), tm=512)


def kernel(x, c, ctx, c_ctx, w_ada, b_ada, g_mix_pre, g_mix_post, g_ffn_pre, g_ffn_post, w_in,
           ret_decay_fwd, ret_decay_bwd, w_out, w_up, conv_w, conv_b, w_down):
    bsz, seq, d = x.shape
    depth = w_ada.shape[0]
    assert depth == 1, "the context stream update of deeper stacks is not implemented"
    assert bsz + 1 <= COND_ROWS and seq % (GRID_W * GRID_W) == 0 and seq // GRID_W == GRID_W
    consts = _dft_constants(seq)
    cond = jnp.concatenate([c, c_ctx[None], jnp.zeros((COND_ROWS - bsz - 1, d), F32)], axis=0)
    layer = 0
    return _layer(x, ctx, cond, consts, w_ada[layer], b_ada[layer], g_mix_pre[layer], g_mix_post[layer],
                  g_ffn_pre[layer], g_ffn_post[layer], w_in[layer], ret_decay_fwd[layer],
                  ret_decay_bwd[layer], w_out[layer], w_up[layer], conv_w[layer], conv_b[layer],
                  w_down[layer])
```

```python
import math

import numpy as np
import jax
import jax.numpy as jnp
from jax import lax
from jax.experimental import pallas as pl
from jax.experimental.pallas import tpu as pltpu

F32 = jnp.float32
BF16 = jnp.bfloat16

RET_HEADS = 4
HEAD_DIM = 128
RET_W = RET_HEADS * HEAD_DIM
FOURIER_GROUPS = 4
GROUP_DIM = 128
FOURIER_W = FOURIER_GROUPS * GROUP_DIM
GRID_W = 64
CHUNK = 128
ROPE_BASE = 10000.0
NORM_EPS = 1e-6
N_MOD = 6
CONV_WIDTH = 3
COND_ROWS = 24
FF_CHUNK = 256
RET_UNROLL = 32
FF_GROUPS = 2
U_PAD = 8
FFT_PITCH = GRID_W + 4
VMEM_LIMIT = 56 * 1024 * 1024


def _silu(x):
    return x / (1.0 + jnp.exp(-x))


def _rms_norm(x, g):
    ms = jnp.mean(x * x, axis=-1, keepdims=True)
    return x * lax.rsqrt(ms + NORM_EPS) * g


def _log_sigmoid(x):
    return jnp.minimum(x, 0.0) - jnp.log(1.0 + jnp.exp(-jnp.abs(x)))


def _params(*sem):
    return pltpu.CompilerParams(dimension_semantics=sem, vmem_limit_bytes=VMEM_LIMIT)


def _adaln_kernel(cond_ref, w_ref, b_ref, o_ref):
    s = _silu(cond_ref[...])
    o_ref[...] = jnp.dot(s, w_ref[...], preferred_element_type=F32) + b_ref[...]


def _adaln(cond, w, b):
    rows, d = cond.shape
    n = w.shape[1]
    tn = d
    return pl.pallas_call(
        _adaln_kernel,
        out_shape=jax.ShapeDtypeStruct((rows, n), F32),
        grid=(n // tn,),
        in_specs=[pl.BlockSpec((rows, d), lambda j: (0, 0)),
                  pl.BlockSpec((d, tn), lambda j: (0, j)),
                  pl.BlockSpec((1, tn), lambda j: (0, j))],
        out_specs=pl.BlockSpec((rows, tn), lambda j: (0, j)),
        compiler_params=_params("arbitrary"),
        name="adaln",
    )(cond, w, b)


def _ctx_kernel(ctx_ref, sh_ref, sc_ref, g_ref, w_ref, decf_ref, decb_ref, sf_ref, sb_ref):
    n_ctx = ctx_ref.shape[1]
    h = _rms_norm(ctx_ref[0], g_ref[...]) * (1.0 + sc_ref[...]) + sh_ref[...]
    kv = jnp.dot(h.astype(BF16), w_ref[...], preferred_element_type=F32)
    lgf = _log_sigmoid(decf_ref[...])
    lgb = _log_sigmoid(decb_ref[...])
    m = lax.broadcasted_iota(jnp.int32, (n_ctx, HEAD_DIM), 0).astype(F32)
    for hd in range(RET_HEADS):
        k = kv[:, hd * HEAD_DIM:(hd + 1) * HEAD_DIM] * (HEAD_DIM ** -0.5)
        v = kv[:, RET_W + hd * HEAD_DIM:RET_W + (hd + 1) * HEAD_DIM].astype(BF16)
        w_f = jnp.exp((n_ctx - 1.0 - m) * lgf[hd:hd + 1, :])
        w_b = jnp.exp(m * lgb[hd:hd + 1, :])
        dn = (((0,), (0,)), ((), ()))
        sf_ref[0, hd] = lax.dot_general((k * w_f).astype(BF16), v, dn, preferred_element_type=F32)
        sb_ref[0, hd] = lax.dot_general((k * w_b).astype(BF16), v, dn, preferred_element_type=F32)


def _ctx_states(ctx, sh, sc, g, w_kv, decf, decb):
    bsz, n_ctx, d = ctx.shape
    st = jax.ShapeDtypeStruct((bsz, RET_HEADS, HEAD_DIM, HEAD_DIM), F32)
    st_spec = pl.BlockSpec((1, RET_HEADS, HEAD_DIM, HEAD_DIM), lambda b: (b, 0, 0, 0))
    vec = pl.BlockSpec((1, d), lambda b: (0, 0))
    dec = pl.BlockSpec((RET_HEADS, HEAD_DIM), lambda b: (0, 0))
    return pl.pallas_call(
        _ctx_kernel,
        out_shape=(st, st),
        grid=(bsz,),
        in_specs=[pl.BlockSpec((1, n_ctx, d), lambda b: (b, 0, 0)), vec, vec, vec,
                  pl.BlockSpec(w_kv.shape, lambda b: (0, 0)), dec, dec],
        out_specs=(st_spec, st_spec),
        compiler_params=_params("arbitrary"),
        name="ctx_state",
    )(ctx, sh, sc, g, w_kv, decf, decb)


def _in_proj_kernel(x_ref, sh_ref, sc_ref, g_ref, w_ref, cos_ref, sin_ref,
                    q_ref, k_ref, v_ref, gate_ref, f_ref):
    h = _rms_norm(x_ref[0], g_ref[...]) * (1.0 + sc_ref[0]) + sh_ref[0]
    proj = jnp.dot(h.astype(BF16), w_ref[...], preferred_element_type=F32)
    cosf = cos_ref[...]
    sinf = sin_ref[...]
    for hd in range(RET_HEADS):
        lo = hd * HEAD_DIM
        q = proj[:, lo:lo + HEAD_DIM]
        k = proj[:, RET_W + lo:RET_W + lo + HEAD_DIM] * (HEAD_DIM ** -0.5)
        q_ref[0, :, lo:lo + HEAD_DIM] = (q * cosf + pltpu.roll(q, HEAD_DIM // 2, 1) * sinf).astype(BF16)
        k_ref[0, :, lo:lo + HEAD_DIM] = (k * cosf + pltpu.roll(k, HEAD_DIM // 2, 1) * sinf).astype(BF16)
    v_ref[0] = proj[:, 2 * RET_W:3 * RET_W].astype(BF16)
    gate_ref[0] = proj[:, 3 * RET_W:4 * RET_W].astype(BF16)
    f_ref[0] = proj[:, 4 * RET_W:].astype(BF16)


def _in_proj(x, sh, sc, g, w_in, cosf, sinf, tm):
    bsz, seq, d = x.shape
    mod = pl.BlockSpec((1, 1, d), lambda b, i: (b, 0, 0))
    tok = lambda w: pl.BlockSpec((1, tm, w), lambda b, i: (b, i, 0))
    out = lambda w: jax.ShapeDtypeStruct((bsz, seq, w), BF16)
    rope = pl.BlockSpec((tm, HEAD_DIM), lambda b, i: (i, 0))
    return pl.pallas_call(
        _in_proj_kernel,
        out_shape=(out(RET_W), out(RET_W), out(RET_W), out(RET_W), out(FOURIER_W)),
        grid=(bsz, seq // tm),
        in_specs=[tok(d), mod, mod, pl.BlockSpec((1, d), lambda b, i: (0, 0)),
                  pl.BlockSpec(w_in.shape, lambda b, i: (0, 0)), rope, rope],
        out_specs=(tok(RET_W), tok(RET_W), tok(RET_W), tok(RET_W), tok(FOURIER_W)),
        compiler_params=_params("arbitrary", "arbitrary"),
        name="in_proj",
    )(x, sh, sc, g, w_in, cosf, sinf)


def _retention_kernel(q_ref, k_ref, v_ref, gate_ref, sf0_ref, sb0_ref, decf_ref, decb_ref,
                      o_ref, kvf_scr, kvb_scr, sf_scr, sb_scr):
    hd = pl.program_id(1)
    n_chunks = q_ref.shape[1] // CHUNK
    lgf = _log_sigmoid(decf_ref[pl.ds(hd, 1), :])
    lgb = _log_sigmoid(decb_ref[pl.ds(hd, 1), :])
    pos = lax.broadcasted_iota(jnp.int32, (CHUNK, HEAD_DIM), 0).astype(F32)
    col = lax.broadcasted_iota(jnp.int32, (CHUNK, CHUNK), 1).astype(F32)
    zeta_f = jnp.exp((CHUNK - 1.0 - pos) * lgf)
    zeta_b = jnp.exp(pos * lgb)
    xi_f = jnp.exp((pos + 1.0) * lgf)
    xi_b = jnp.exp((CHUNK - pos) * lgb)
    diff = pos - col
    decay = jnp.exp(jnp.where(diff >= 0, diff * lgf, -diff * lgb))
    cdec_f = jnp.exp(CHUNK * lgf)
    cdec_b = jnp.exp(CHUNK * lgb)
    tn = (((0,), (0,)), ((), ()))
    nt = (((1,), (1,)), ((), ()))

    def chunk_kv(n, carry):
        rows = pl.ds(pl.multiple_of(n * CHUNK, CHUNK), CHUNK)
        k = k_ref[0, rows, :].astype(F32)
        v = v_ref[0, rows, :]
        kvf_scr[n] = lax.dot_general((k * zeta_f).astype(BF16), v, tn, preferred_element_type=F32)
        kvb_scr[n] = lax.dot_general((k * zeta_b).astype(BF16), v, tn, preferred_element_type=F32)
        return carry

    lax.fori_loop(0, n_chunks, chunk_kv, 0, unroll=RET_UNROLL)

    def scan_f(n, state):
        sf_scr[n] = state.astype(BF16)
        return state * cdec_f + kvf_scr[n]

    lax.fori_loop(0, n_chunks, scan_f, sf0_ref[0, 0])

    def scan_b(i, state):
        n = n_chunks - 1 - i
        sb_scr[n] = state.astype(BF16)
        return state * cdec_b + kvb_scr[n]

    lax.fori_loop(0, n_chunks, scan_b, sb0_ref[0, 0])

    def chunk_out(n, carry):
        rows = pl.ds(pl.multiple_of(n * CHUNK, CHUNK), CHUNK)
        q = q_ref[0, rows, :]
        k = k_ref[0, rows, :]
        v = v_ref[0, rows, :]
        scores = lax.dot_general(q, k, nt, preferred_element_type=F32) * decay
        qf = q.astype(F32)
        o = jnp.dot(scores.astype(BF16), v, preferred_element_type=F32)
        o = o + jnp.dot((qf * xi_f).astype(BF16), sf_scr[n], preferred_element_type=F32)
        o = o + jnp.dot((qf * xi_b).astype(BF16), sb_scr[n], preferred_element_type=F32)
        o = o * lax.rsqrt(jnp.mean(o * o, axis=-1, keepdims=True) + NORM_EPS)
        o_ref[0, rows, :] = (o * _silu(gate_ref[0, rows, :].astype(F32))).astype(BF16)
        return carry

    lax.fori_loop(0, n_chunks, chunk_out, 0, unroll=RET_UNROLL)


def _retention(q, k, v, gate, sf0, sb0, decf, decb):
    bsz, seq, _ = q.shape
    n_chunks = seq // CHUNK
    tok = pl.BlockSpec((1, seq, HEAD_DIM), lambda b, h: (b, 0, h))
    st = pl.BlockSpec((1, 1, HEAD_DIM, HEAD_DIM), lambda b, h: (b, h, 0, 0))
    dec = pl.BlockSpec((RET_HEADS, HEAD_DIM), lambda b, h: (0, 0))
    return pl.pallas_call(
        _retention_kernel,
        out_shape=jax.ShapeDtypeStruct((bsz, seq, RET_W), BF16),
        grid=(bsz, RET_HEADS),
        in_specs=[tok, tok, tok, tok, st, st, dec, dec],
        out_specs=tok,
        scratch_shapes=[pltpu.VMEM((n_chunks, HEAD_DIM, HEAD_DIM), F32),
                        pltpu.VMEM((n_chunks, HEAD_DIM, HEAD_DIM), F32),
                        pltpu.VMEM((n_chunks, HEAD_DIM, HEAD_DIM), BF16),
                        pltpu.VMEM((n_chunks, HEAD_DIM, HEAD_DIM), BF16)],
        compiler_params=_params("arbitrary", "arbitrary"),
        name="retention",
    )(q, k, v, gate, sf0, sb0, decf, decb)


def _fourier_kernel(f_ref, w1_ref, twc_ref, tws_ref, w2_ref, wc_ref, o_ref,
                    xs, ar_s, ai_s, xr_s, xi_s, lhs_s):
    gh = xs.shape[0] // FFT_PITCH
    for t1 in range(gh):
        xs[pl.ds(t1 * FFT_PITCH, GRID_W), :] = f_ref[0, pl.ds(t1 * GRID_W, GRID_W), :].astype(F32)
    for t2 in range(0, GRID_W, 2):
        x = jnp.concatenate([xs[pl.ds(t2 + u, gh, stride=FFT_PITCH), :] for u in range(2)], axis=1)
        r = jnp.dot(w1_ref[...], x.astype(BF16), preferred_element_type=F32)
        for u in range(2):
            lanes = slice(u * GROUP_DIM, (u + 1) * GROUP_DIM)
            ar_s[pl.ds(t2 + u, gh, stride=FFT_PITCH), :] = r[:gh, lanes]
            ai_s[pl.ds(t2 + u, gh, stride=FFT_PITCH), :] = r[gh:, lanes]
    for k1 in range(0, gh, 2):
        cols = []
        for u in range(2):
            rows = pl.ds((k1 + u) * FFT_PITCH, GRID_W)
            tw_rows = pl.ds((k1 + u) * GRID_W, GRID_W)
            ar, ai = ar_s[rows, :], ai_s[rows, :]
            twc, tws = twc_ref[tw_rows, :], tws_ref[tw_rows, :]
            cols.append(jnp.concatenate([ar * twc + ai * tws, ai * twc - ar * tws], axis=0))
        r = jnp.dot(w2_ref[...], jnp.concatenate(cols, axis=1).astype(BF16), preferred_element_type=F32)
        for u in range(2):
            lanes = slice(u * GROUP_DIM, (u + 1) * GROUP_DIM)
            xr_s[pl.ds(k1 + u, GRID_W, stride=FFT_PITCH), :] = r[:GRID_W, lanes]
            xi_s[pl.ds(k1 + u, GRID_W, stride=FFT_PITCH), :] = r[GRID_W:, lanes]
    for k2 in range(GRID_W):
        rows = pl.ds(k2 * FFT_PITCH, gh)
        lhs_s[pl.ds(k2 * gh, gh), :GROUP_DIM] = xr_s[rows, :].astype(BF16)
        lhs_s[pl.ds(k2 * gh, gh), GROUP_DIM:] = xi_s[rows, :].astype(BF16)
    o_ref[0] = jnp.dot(lhs_s[...], wc_ref[...], preferred_element_type=F32).astype(BF16)


def _fourier(f, consts):
    bsz, seq, width = f.shape
    gh = seq // GRID_W
    tok = pl.BlockSpec((1, seq, GROUP_DIM), lambda b, g: (b, 0, g))
    const = lambda a: pl.BlockSpec(a.shape, lambda b, g: (0,) * a.ndim)
    names = ("w1", "twc", "tws", "w2", "wc")
    pitched = pltpu.VMEM((gh * FFT_PITCH, GROUP_DIM), F32)
    return pl.pallas_call(
        _fourier_kernel,
        out_shape=jax.ShapeDtypeStruct((bsz, seq, width), BF16),
        grid=(bsz, width // GROUP_DIM),
        in_specs=[tok] + [const(consts[n]) for n in names],
        out_specs=tok,
        scratch_shapes=[pitched] * 5 + [pltpu.VMEM((seq, 2 * GROUP_DIM), BF16)],
        compiler_params=_params("arbitrary", "arbitrary"),
        name="fourier",
    )(f, *[consts[n] for n in names])


def _dft_constants(seq):
    gh = seq // GRID_W
    def cs(n_rows, n_cols, period):
        ang = 2.0 * np.pi * np.outer(np.arange(n_rows), np.arange(n_cols)) / period
        return np.cos(ang), np.sin(ang)
    c1, s1 = cs(gh, gh, gh)
    w1 = np.concatenate([c1, -s1], axis=0) / gh
    c2, s2 = cs(GRID_W, GRID_W, GRID_W)
    w2 = np.block([[c2, s2], [-s2, c2]]) / math.sqrt(GRID_W * gh) * gh
    cc, sc = cs(GROUP_DIM, GROUP_DIM, GROUP_DIM)
    wc = np.concatenate([cc, sc], axis=0) / math.sqrt(GROUP_DIM)
    k1 = np.repeat(np.arange(gh), GRID_W)
    t2 = np.tile(np.arange(GRID_W), gh)
    ang = 2.0 * np.pi * (k1 * t2 % seq) / seq
    twc = np.broadcast_to(np.cos(ang)[:, None], (seq, GROUP_DIM))
    tws = np.broadcast_to(np.sin(ang)[:, None], (seq, GROUP_DIM))
    t = np.arange(seq)
    n_freq = HEAD_DIM // 4
    freqs = ROPE_BASE ** (-np.arange(n_freq, dtype=np.float32) / n_freq)
    rang = np.concatenate([(t // GRID_W).astype(np.float32)[:, None] * freqs,
                           (t % GRID_W).astype(np.float32)[:, None] * freqs], axis=-1).astype(np.float32)
    cosf = np.concatenate([np.cos(rang), np.cos(rang)], axis=-1)
    sinf = np.concatenate([-np.sin(rang), np.sin(rang)], axis=-1)
    as_f32 = lambda a: jnp.asarray(np.ascontiguousarray(a), F32)
    as_bf16 = lambda a: jnp.asarray(np.ascontiguousarray(a), F32).astype(BF16)
    return dict(w1=as_bf16(w1), w2=as_bf16(w2), wc=as_bf16(wc), twc=as_f32(twc), tws=as_f32(tws),
                cosf=as_f32(cosf), sinf=as_f32(sinf))


def _out_ffn_kernel(x_ref, ret_ref, four_ref, mod_ref, gpost_ref, gpre_ref, gfpost_ref,
                    wo_ref, wup_ref, edge_ref, mid_ref, wd_ref, o_ref, h_ref, u_ref, act_ref):
    n_groups = u_ref.shape[0]
    th = x_ref.shape[1] // n_groups
    d_ff = wd_ref.shape[0]
    gt_a, sh_f, sc_f, gt_f = mod_ref[0, 0:1], mod_ref[0, 1:2], mod_ref[0, 2:3], mod_ref[0, 3:4]
    groups = [slice(s * th, (s + 1) * th) for s in range(n_groups)]

    for s, rows in enumerate(groups):
        mix = jnp.dot(ret_ref[0, rows, :], wo_ref[:RET_W, :], preferred_element_type=F32)
        mix = mix + jnp.dot(four_ref[0, rows, :], wo_ref[RET_W:, :], preferred_element_type=F32)
        x1 = x_ref[0, rows, :] + gt_a * _rms_norm(mix, gpost_ref[...])
        o_ref[0, rows, :] = x1
        h_ref[rows, :] = (_rms_norm(x1, gpre_ref[...]) * (1.0 + sc_f) + sh_f).astype(BF16)
        for slot in range(u_ref.shape[1]):
            u_ref[s, slot, 0:U_PAD, :] = jnp.zeros((U_PAD, FF_CHUNK), F32)
            u_ref[s, slot, U_PAD + th:, :] = jnp.zeros((U_PAD, FF_CHUNK), F32)

    def conv(s, slot, lo):
        taps = lambda a: a.reshape(th // GRID_W, GRID_W, FF_CHUNK)
        prev = taps(u_ref[s, slot, U_PAD - 1:U_PAD - 1 + th, :]) * edge_ref[0, :, lo:lo + FF_CHUNK]
        nxt = taps(u_ref[s, slot, U_PAD + 1:U_PAD + 1 + th, :]) * edge_ref[1, :, lo:lo + FF_CHUNK]
        cur = u_ref[s, slot, U_PAD:U_PAD + th, :] * mid_ref[0:1, lo:lo + FF_CHUNK]
        return (prev + nxt).reshape(th, FF_CHUNK) + cur + mid_ref[1:2, lo:lo + FF_CHUNK]

    for j in range(d_ff // FF_CHUNK):
        cols = (j * FF_CHUNK, d_ff + j * FF_CHUNK)
        slots = (2 * (j % 2), 2 * (j % 2) + 1)
        for s, rows in enumerate(groups):
            h = h_ref[rows, :]
            for slot, lo in zip(slots, cols):
                u_ref[s, slot, U_PAD:U_PAD + th, :] = jnp.dot(h, wup_ref[:, lo:lo + FF_CHUNK],
                                                              preferred_element_type=F32)
            a = conv(s, slots[0], cols[0])
            b = conv(s, slots[1], cols[1])
            act_ref[rows, j * FF_CHUNK:(j + 1) * FF_CHUNK] = (_silu(a) * b).astype(BF16)

    for rows in groups:
        ffn = jnp.dot(act_ref[rows, :], wd_ref[...], preferred_element_type=F32)
        o_ref[0, rows, :] = o_ref[0, rows, :] + gt_f * _rms_norm(ffn, gfpost_ref[...])


def _out_ffn(x, ret, four, mod, gpost, gpre, gfpost, wo, wup, edge, mid, wd, tm):
    bsz, seq, d = x.shape
    d_ff = wd.shape[0]
    tok = lambda w: pl.BlockSpec((1, tm, w), lambda b, i: (b, i, 0))
    vec = pl.BlockSpec((1, d), lambda b, i: (0, 0))
    const = lambda a: pl.BlockSpec(a.shape, lambda b, i: (0,) * a.ndim, pipeline_mode=pl.Buffered(1))
    return pl.pallas_call(
        _out_ffn_kernel,
        out_shape=jax.ShapeDtypeStruct((bsz, seq, d), F32),
        grid=(bsz, seq // tm),
        in_specs=[tok(d), tok(RET_W), tok(FOURIER_W),
                  pl.BlockSpec((1, 4, d), lambda b, i: (b, 0, 0)), vec, vec, vec,
                  const(wo), const(wup), const(edge), const(mid), const(wd)],
        out_specs=tok(d),
        scratch_shapes=[pltpu.VMEM((tm, d), BF16),
                        pltpu.VMEM((FF_GROUPS, 4, tm // FF_GROUPS + 2 * U_PAD, FF_CHUNK), F32),
                        pltpu.VMEM((tm, d_ff), BF16)],
        compiler_params=_params("arbitrary", "arbitrary"),
        name="out_ffn",
    )(x, ret, four, mod, gpost, gpre, gfpost, wo, wup, edge, mid, wd)


def _layer(x, ctx, cond, consts, w_ada, b_ada, g_mix_pre, g_mix_post, g_ffn_pre, g_ffn_post,
           w_in, dec_f, dec_b, w_out, w_up, conv_w, conv_b, w_down):
    bsz, seq, d = x.shape
    d_ff = w_down.shape[0]
    row = lambda v: v.reshape(1, -1)

    mod = _adaln(cond, w_ada, row(b_ada))
    mod = mod.reshape(COND_ROWS, N_MOD, d)
    lat = mod[:bsz]
    cmod = mod[bsz]

    decf = jnp.broadcast_to(dec_f[:, None], (RET_HEADS, HEAD_DIM))
    decb = jnp.broadcast_to(dec_b[:, None], (RET_HEADS, HEAD_DIM))
    w_in_bf = w_in.astype(BF16)
    sf0, sb0 = _ctx_states(ctx, cmod[0:1], cmod[1:2], row(g_mix_pre),
                           w_in_bf[:, RET_W:3 * RET_W], decf, decb)

    q, k, v, gate, f = _in_proj(x, lat[:, 0:1], lat[:, 1:2], row(g_mix_pre), w_in_bf,
                                consts["cosf"], consts["sinf"], tm=512)
    ret = _retention(q, k, v, gate, sf0, sb0, decf, decb)
    four = _fourier(f, consts)

    col = jnp.arange(GRID_W)[:, None]
    edge = jnp.stack([jnp.where(col != 0, conv_w[0][None, :], 0.0),
                      jnp.where(col != GRID_W - 1, conv_w[2][None, :], 0.0)])
    mid = jnp.concatenate([conv_w[1][None], conv_b[None], jnp.zeros((6, 2 * d_ff), F32)], axis=0)
    return _out_ffn(x, ret, four, lat[:, 2:6], row(g_mix_post), row(g_ffn_pre), row(g_ffn_post),
                    w_out.astype(BF16), w_up.astype(BF16), edge, mid, w_down.astype(BF16), tm=512)


def kernel(x, c, ctx, c_ctx, w_ada, b_ada, g_mix_pre, g_mix_post, g_ffn_pre, g_ffn_post, w_in,
           ret_decay_fwd, ret_decay_bwd, w_out, w_up, conv_w, conv_b, w_down):
    bsz, seq, d = x.shape
    depth = w_ada.shape[0]
    assert depth == 1, "the context stream update of deeper stacks is not implemented"
    assert bsz + 1 <= COND_ROWS and seq % (GRID_W * GRID_W) == 0 and seq // GRID_W == GRID_W
    consts = _dft_constants(seq)
    cond = jnp.concatenate([c, c_ctx[None], jnp.zeros((COND_ROWS - bsz - 1, d), F32)], axis=0)
    layer = 0
    return _layer(x, ctx, cond, consts, w_ada[layer], b_ada[layer], g_mix_pre[layer], g_mix_post[layer],
                  g_ffn_pre[layer], g_ffn_post[layer], w_in[layer], ret_decay_fwd[layer],
                  ret_decay_bwd[layer], w_out[layer], w_up[layer], conv_w[layer], conv_b[layer],
                  w_down[layer])
```

```python
import math

import numpy as np
import jax
import jax.numpy as jnp
from jax import lax
from jax.experimental import pallas as pl
from jax.experimental.pallas import tpu as pltpu

F32 = jnp.float32
BF16 = jnp.bfloat16

RET_HEADS = 4
HEAD_DIM = 128
RET_W = RET_HEADS * HEAD_DIM
FOURIER_GROUPS = 4
GROUP_DIM = 128
FOURIER_W = FOURIER_GROUPS * GROUP_DIM
LANES = 128
GRID_W = 64
CHUNK = 128
ROPE_BASE = 10000.0
NORM_EPS = 1e-6
N_MOD = 6
CONV_WIDTH = 3
COND_ROWS = 24
FF_CHUNK = 256
RET_UNROLL = 32
IN_GROUPS = 2
FF_GROUPS = 2
U_PAD = 8
FFT_PITCH = GRID_W + 4
VMEM_LIMIT = 56 * 1024 * 1024


def _silu(x):
    return x / (1.0 + jnp.exp(-x))


def _rms_scale(x):
    return lax.rsqrt(jnp.mean(x * x, axis=-1, keepdims=True) + NORM_EPS)


def _rms_norm(x, g):
    return x * _rms_scale(x) * g


def _log_sigmoid(x):
    return jnp.minimum(x, 0.0) - jnp.log(1.0 + jnp.exp(-jnp.abs(x)))


def _params(*sem):
    return pltpu.CompilerParams(dimension_semantics=sem, vmem_limit_bytes=VMEM_LIMIT)


def _adaln_kernel(cond_ref, w_ref, b_ref, o_ref):
    s = _silu(cond_ref[...])
    o_ref[...] = jnp.dot(s, w_ref[...], preferred_element_type=F32) + b_ref[...]


def _adaln(cond, w, b):
    rows, d = cond.shape
    n = w.shape[1]
    tn = d
    return pl.pallas_call(
        _adaln_kernel,
        out_shape=jax.ShapeDtypeStruct((rows, n), F32),
        grid=(n // tn,),
        in_specs=[pl.BlockSpec((rows, d), lambda j: (0, 0)),
                  pl.BlockSpec((d, tn), lambda j: (0, j)),
                  pl.BlockSpec((1, tn), lambda j: (0, j))],
        out_specs=pl.BlockSpec((rows, tn), lambda j: (0, j)),
        compiler_params=_params("arbitrary"),
        name="adaln",
    )(cond, w, b)


def _ctx_kernel(ctx_ref, sh_ref, sc_ref, g_ref, w_ref, decf_ref, decb_ref, sf_ref, sb_ref):
    n_ctx = ctx_ref.shape[1]
    h = _rms_norm(ctx_ref[0], g_ref[...]) * (1.0 + sc_ref[...]) + sh_ref[...]
    kv = jnp.dot(h.astype(BF16), w_ref[...], preferred_element_type=F32)
    lgf = _log_sigmoid(decf_ref[...])
    lgb = _log_sigmoid(decb_ref[...])
    m = lax.broadcasted_iota(jnp.int32, (n_ctx, HEAD_DIM), 0).astype(F32)
    for hd in range(RET_HEADS):
        k = kv[:, hd * HEAD_DIM:(hd + 1) * HEAD_DIM] * (HEAD_DIM ** -0.5)
        v = kv[:, RET_W + hd * HEAD_DIM:RET_W + (hd + 1) * HEAD_DIM].astype(BF16)
        w_f = jnp.exp((n_ctx - 1.0 - m) * lgf[hd:hd + 1, :])
        w_b = jnp.exp(m * lgb[hd:hd + 1, :])
        dn = (((0,), (0,)), ((), ()))
        sf_ref[0, hd] = lax.dot_general((k * w_f).astype(BF16), v, dn, preferred_element_type=F32)
        sb_ref[0, hd] = lax.dot_general((k * w_b).astype(BF16), v, dn, preferred_element_type=F32)


def _ctx_states(ctx, sh, sc, g, w_kv, decf, decb):
    bsz, n_ctx, d = ctx.shape
    st = jax.ShapeDtypeStruct((bsz, RET_HEADS, HEAD_DIM, HEAD_DIM), F32)
    st_spec = pl.BlockSpec((1, RET_HEADS, HEAD_DIM, HEAD_DIM), lambda b: (b, 0, 0, 0))
    vec = pl.BlockSpec((1, d), lambda b: (0, 0))
    dec = pl.BlockSpec((RET_HEADS, HEAD_DIM), lambda b: (0, 0))
    return pl.pallas_call(
        _ctx_kernel,
        out_shape=(st, st),
        grid=(bsz,),
        in_specs=[pl.BlockSpec((1, n_ctx, d), lambda b: (b, 0, 0)), vec, vec, vec,
                  pl.BlockSpec(w_kv.shape, lambda b: (0, 0)), dec, dec],
        out_specs=(st_spec, st_spec),
        compiler_params=_params("arbitrary"),
        name="ctx_state",
    )(ctx, sh, sc, g, w_kv, decf, decb)


def _in_proj_kernel(x_ref, sh_ref, sc_ref, g_ref, w_ref, cos_ref, sin_ref,
                    q_ref, k_ref, v_ref, gate_ref, f_ref):
    th = x_ref.shape[1] // IN_GROUPS
    gain = g_ref[...] * (1.0 + sc_ref[0])
    for s in range(IN_GROUPS):
        rows = slice(s * th, (s + 1) * th)
        x = x_ref[0, rows, :]
        h = x * _rms_scale(x) * gain + sh_ref[0]
        proj = jnp.dot(h.astype(BF16), w_ref[...], preferred_element_type=F32)
        cosf = cos_ref[rows, :]
        sinf = sin_ref[rows, :]
        for hd in range(RET_HEADS):
            lo = hd * HEAD_DIM
            q = proj[:, lo:lo + HEAD_DIM]
            k = proj[:, RET_W + lo:RET_W + lo + HEAD_DIM] * (HEAD_DIM ** -0.5)
            q_ref[0, rows, lo:lo + HEAD_DIM] = (q * cosf + pltpu.roll(q, HEAD_DIM // 2, 1) * sinf).astype(BF16)
            k_ref[0, rows, lo:lo + HEAD_DIM] = (k * cosf + pltpu.roll(k, HEAD_DIM // 2, 1) * sinf).astype(BF16)
        v_ref[0, rows, :] = proj[:, 2 * RET_W:3 * RET_W].astype(BF16)
        gate_ref[0, rows, :] = proj[:, 3 * RET_W:4 * RET_W].astype(BF16)
        f_ref[0, rows, :] = proj[:, 4 * RET_W:].astype(BF16)


def _in_proj(x, sh, sc, g, w_in, cosf, sinf, tm):
    bsz, seq, d = x.shape
    mod = pl.BlockSpec((1, 1, d), lambda b, i: (b, 0, 0))
    tok = lambda w: pl.BlockSpec((1, tm, w), lambda b, i: (b, i, 0))
    out = lambda w: jax.ShapeDtypeStruct((bsz, seq, w), BF16)
    rope = pl.BlockSpec((tm, HEAD_DIM), lambda b, i: (i, 0))
    return pl.pallas_call(
        _in_proj_kernel,
        out_shape=(out(RET_W), out(RET_W), out(RET_W), out(RET_W), out(FOURIER_W)),
        grid=(bsz, seq // tm),
        in_specs=[tok(d), mod, mod, pl.BlockSpec((1, d), lambda b, i: (0, 0)),
                  pl.BlockSpec(w_in.shape, lambda b, i: (0, 0)), rope, rope],
        out_specs=(tok(RET_W), tok(RET_W), tok(RET_W), tok(RET_W), tok(FOURIER_W)),
        compiler_params=_params("arbitrary", "arbitrary"),
        name="in_proj",
    )(x, sh, sc, g, w_in, cosf, sinf)


def _retention_kernel(q_ref, k_ref, v_ref, gate_ref, sf0_ref, sb0_ref, decf_ref, decb_ref,
                      o_ref, kvf_scr, kvb_scr, sf_scr, sb_scr):
    hd = pl.program_id(1)
    n_chunks = q_ref.shape[1] // CHUNK
    lgf = _log_sigmoid(decf_ref[pl.ds(hd, 1), :])
    lgb = _log_sigmoid(decb_ref[pl.ds(hd, 1), :])
    pos = lax.broadcasted_iota(jnp.int32, (CHUNK, HEAD_DIM), 0).astype(F32)
    col = lax.broadcasted_iota(jnp.int32, (CHUNK, CHUNK), 1).astype(F32)
    zeta_f = jnp.exp((CHUNK - 1.0 - pos) * lgf)
    zeta_b = jnp.exp(pos * lgb)
    xi_f = jnp.exp((pos + 1.0) * lgf)
    xi_b = jnp.exp((CHUNK - pos) * lgb)
    diff = pos - col
    decay = jnp.exp(jnp.where(diff >= 0, diff * lgf, -diff * lgb))
    cdec_f = jnp.exp(CHUNK * lgf)
    cdec_b = jnp.exp(CHUNK * lgb)
    tn = (((0,), (0,)), ((), ()))
    nt = (((1,), (1,)), ((), ()))

    def chunk_kv(n, carry):
        rows = pl.ds(pl.multiple_of(n * CHUNK, CHUNK), CHUNK)
        k = k_ref[0, rows, :].astype(F32)
        v = v_ref[0, rows, :]
        kvf_scr[n] = lax.dot_general((k * zeta_f).astype(BF16), v, tn, preferred_element_type=F32)
        kvb_scr[n] = lax.dot_general((k * zeta_b).astype(BF16), v, tn, preferred_element_type=F32)
        return carry

    lax.fori_loop(0, n_chunks, chunk_kv, 0, unroll=RET_UNROLL)

    def scan_f(n, state):
        sf_scr[n] = state.astype(BF16)
        return state * cdec_f + kvf_scr[n]

    lax.fori_loop(0, n_chunks, scan_f, sf0_ref[0, 0])

    def scan_b(i, state):
        n = n_chunks - 1 - i
        sb_scr[n] = state.astype(BF16)
        return state * cdec_b + kvb_scr[n]

    lax.fori_loop(0, n_chunks, scan_b, sb0_ref[0, 0])

    def chunk_out(n, carry):
        rows = pl.ds(pl.multiple_of(n * CHUNK, CHUNK), CHUNK)
        q = q_ref[0, rows, :]
        k = k_ref[0, rows, :]
        v = v_ref[0, rows, :]
        scores = lax.dot_general(q, k, nt, preferred_element_type=F32) * decay
        qf = q.astype(F32)
        o = jnp.dot(scores.astype(BF16), v, preferred_element_type=F32)
        o = o + jnp.dot((qf * xi_f).astype(BF16), sf_scr[n], preferred_element_type=F32)
        o = o + jnp.dot((qf * xi_b).astype(BF16), sb_scr[n], preferred_element_type=F32)
        o = o * lax.rsqrt(jnp.mean(o * o, axis=-1, keepdims=True) + NORM_EPS)
        o_ref[0, rows, :] = (o * _silu(gate_ref[0, rows, :].astype(F32))).astype(BF16)
        return carry

    lax.fori_loop(0, n_chunks, chunk_out, 0, unroll=RET_UNROLL)


def _retention(q, k, v, gate, sf0, sb0, decf, decb):
    bsz, seq, _ = q.shape
    n_chunks = seq // CHUNK
    tok = pl.BlockSpec((1, seq, HEAD_DIM), lambda b, h: (b, 0, h))
    st = pl.BlockSpec((1, 1, HEAD_DIM, HEAD_DIM), lambda b, h: (b, h, 0, 0))
    dec = pl.BlockSpec((RET_HEADS, HEAD_DIM), lambda b, h: (0, 0))
    return pl.pallas_call(
        _retention_kernel,
        out_shape=jax.ShapeDtypeStruct((bsz, seq, RET_W), BF16),
        grid=(bsz, RET_HEADS),
        in_specs=[tok, tok, tok, tok, st, st, dec, dec],
        out_specs=tok,
        scratch_shapes=[pltpu.VMEM((n_chunks, HEAD_DIM, HEAD_DIM), F32),
                        pltpu.VMEM((n_chunks, HEAD_DIM, HEAD_DIM), F32),
                        pltpu.VMEM((n_chunks, HEAD_DIM, HEAD_DIM), BF16),
                        pltpu.VMEM((n_chunks, HEAD_DIM, HEAD_DIM), BF16)],
        compiler_params=_params("arbitrary", "arbitrary"),
        name="retention",
    )(q, k, v, gate, sf0, sb0, decf, decb)


def _fourier_kernel(f_ref, w1_ref, twc_ref, tws_ref, w2_ref, wc_ref, o_ref,
                    xs, ar_s, ai_s, xr_s, xi_s, lhs_s):
    gh = xs.shape[0] // FFT_PITCH
    for t1 in range(gh):
        xs[pl.ds(t1 * FFT_PITCH, GRID_W), :] = f_ref[0, pl.ds(t1 * GRID_W, GRID_W), :].astype(F32)
    for t2 in range(0, GRID_W, 2):
        x = jnp.concatenate([xs[pl.ds(t2 + u, gh, stride=FFT_PITCH), :] for u in range(2)], axis=1)
        r = jnp.dot(w1_ref[...], x.astype(BF16), preferred_element_type=F32)
        for u in range(2):
            lanes = slice(u * GROUP_DIM, (u + 1) * GROUP_DIM)
            ar_s[pl.ds(t2 + u, gh, stride=FFT_PITCH), :] = r[:gh, lanes]
            ai_s[pl.ds(t2 + u, gh, stride=FFT_PITCH), :] = r[gh:, lanes]
    for k1 in range(0, gh, 2):
        cols = []
        for u in range(2):
            rows = pl.ds((k1 + u) * FFT_PITCH, GRID_W)
            tw_rows = pl.ds((k1 + u) * GRID_W, GRID_W)
            ar, ai = ar_s[rows, :], ai_s[rows, :]
            twc, tws = twc_ref[tw_rows, :], tws_ref[tw_rows, :]
            cols.append(jnp.concatenate([ar * twc + ai * tws, ai * twc - ar * tws], axis=0))
        r = jnp.dot(w2_ref[...], jnp.concatenate(cols, axis=1).astype(BF16), preferred_element_type=F32)
        for u in range(2):
            lanes = slice(u * GROUP_DIM, (u + 1) * GROUP_DIM)
            xr_s[pl.ds(k1 + u, GRID_W, stride=FFT_PITCH), :] = r[:GRID_W, lanes]
            xi_s[pl.ds(k1 + u, GRID_W, stride=FFT_PITCH), :] = r[GRID_W:, lanes]
    for k2 in range(GRID_W):
        rows = pl.ds(k2 * FFT_PITCH, gh)
        lhs_s[pl.ds(k2 * gh, gh), :GROUP_DIM] = xr_s[rows, :].astype(BF16)
        lhs_s[pl.ds(k2 * gh, gh), GROUP_DIM:] = xi_s[rows, :].astype(BF16)
    o_ref[0] = jnp.dot(lhs_s[...], wc_ref[...], preferred_element_type=F32).astype(BF16)


def _fourier(f, consts):
    bsz, seq, width = f.shape
    gh = seq // GRID_W
    tok = pl.BlockSpec((1, seq, GROUP_DIM), lambda b, g: (b, 0, g))
    const = lambda a: pl.BlockSpec(a.shape, lambda b, g: (0,) * a.ndim)
    names = ("w1", "twc", "tws", "w2", "wc")
    pitched = pltpu.VMEM((gh * FFT_PITCH, GROUP_DIM), F32)
    return pl.pallas_call(
        _fourier_kernel,
        out_shape=jax.ShapeDtypeStruct((bsz, seq, width), BF16),
        grid=(bsz, width // GROUP_DIM),
        in_specs=[tok] + [const(consts[n]) for n in names],
        out_specs=tok,
        scratch_shapes=[pitched] * 5 + [pltpu.VMEM((seq, 2 * GROUP_DIM), BF16)],
        compiler_params=_params("arbitrary", "arbitrary"),
        name="fourier",
    )(f, *[consts[n] for n in names])


def _dft_constants(seq):
    gh = seq // GRID_W
    def cs(n_rows, n_cols, period):
        ang = 2.0 * np.pi * np.outer(np.arange(n_rows), np.arange(n_cols)) / period
        return np.cos(ang), np.sin(ang)
    c1, s1 = cs(gh, gh, gh)
    w1 = np.concatenate([c1, -s1], axis=0) / gh
    c2, s2 = cs(GRID_W, GRID_W, GRID_W)
    w2 = np.block([[c2, s2], [-s2, c2]]) / math.sqrt(GRID_W * gh) * gh
    cc, sc = cs(GROUP_DIM, GROUP_DIM, GROUP_DIM)
    wc = np.concatenate([cc, sc], axis=0) / math.sqrt(GROUP_DIM)
    k1 = np.repeat(np.arange(gh), GRID_W)
    t2 = np.tile(np.arange(GRID_W), gh)
    ang = 2.0 * np.pi * (k1 * t2 % seq) / seq
    twc = np.broadcast_to(np.cos(ang)[:, None], (seq, GROUP_DIM))
    tws = np.broadcast_to(np.sin(ang)[:, None], (seq, GROUP_DIM))
    t = np.arange(seq)
    n_freq = HEAD_DIM // 4
    freqs = ROPE_BASE ** (-np.arange(n_freq, dtype=np.float32) / n_freq)
    rang = np.concatenate([(t // GRID_W).astype(np.float32)[:, None] * freqs,
                           (t % GRID_W).astype(np.float32)[:, None] * freqs], axis=-1).astype(np.float32)
    cosf = np.concatenate([np.cos(rang), np.cos(rang)], axis=-1)
    sinf = np.concatenate([-np.sin(rang), np.sin(rang)], axis=-1)
    as_f32 = lambda a: jnp.asarray(np.ascontiguousarray(a), F32)
    as_bf16 = lambda a: jnp.asarray(np.ascontiguousarray(a), F32).astype(BF16)
    return dict(w1=as_bf16(w1), w2=as_bf16(w2), wc=as_bf16(wc), twc=as_f32(twc), tws=as_f32(tws),
                cosf=as_f32(cosf), sinf=as_f32(sinf))


def _out_ffn_kernel(x_ref, ret_ref, four_ref, mod_ref, gpost_ref, gpre_ref, gfpost_ref,
                    wo_ref, wup_ref, edge_ref, mid_ref, wd_ref, o_ref, h_ref, u_ref, act_ref):
    n_groups = u_ref.shape[0]
    th = x_ref.shape[1] // n_groups
    d_ff = wd_ref.shape[0]
    gt_a, sh_f, sc_f, gt_f = mod_ref[0, 0:1], mod_ref[0, 1:2], mod_ref[0, 2:3], mod_ref[0, 3:4]
    groups = [slice(s * th, (s + 1) * th) for s in range(n_groups)]
    gain_a = gpost_ref[...] * gt_a
    gain_f = gpre_ref[...] * (1.0 + sc_f)
    gain_o = gfpost_ref[...] * gt_f

    for s, rows in enumerate(groups):
        mix = jnp.dot(ret_ref[0, rows, :], wo_ref[:RET_W, :], preferred_element_type=F32)
        mix = mix + jnp.dot(four_ref[0, rows, :], wo_ref[RET_W:, :], preferred_element_type=F32)
        x1 = x_ref[0, rows, :] + mix * _rms_scale(mix) * gain_a
        o_ref[0, rows, :] = x1
        h_ref[rows, :] = (x1 * _rms_scale(x1) * gain_f + sh_f).astype(BF16)
        u_ref[s, :, :, 0:U_PAD, :] = jnp.zeros(u_ref.shape[1:3] + (U_PAD, LANES), F32)
        u_ref[s, :, :, U_PAD + th:, :] = jnp.zeros(u_ref.shape[1:3] + (U_PAD, LANES), F32)

    lane_tiles = FF_CHUNK // LANES

    def conv(s, slot, lt, lo):
        taps = lambda a: a.reshape(th // GRID_W, GRID_W, LANES)
        prev = taps(u_ref[s, slot, lt, U_PAD - 1:U_PAD - 1 + th, :]) * edge_ref[0, :, lo:lo + LANES]
        nxt = taps(u_ref[s, slot, lt, U_PAD + 1:U_PAD + 1 + th, :]) * edge_ref[1, :, lo:lo + LANES]
        cur = u_ref[s, slot, lt, U_PAD:U_PAD + th, :] * mid_ref[0:1, lo:lo + LANES]
        return (prev + nxt).reshape(th, LANES) + cur + mid_ref[1:2, lo:lo + LANES]

    for j in range(d_ff // FF_CHUNK):
        cols = (j * FF_CHUNK, d_ff + j * FF_CHUNK)
        slots = (2 * j, 2 * j + 1)
        for s, rows in enumerate(groups):
            h = h_ref[rows, :]
            for slot, lo in zip(slots, cols):
                u = jnp.dot(h, wup_ref[:, lo:lo + FF_CHUNK], preferred_element_type=F32)
                for lt in range(lane_tiles):
                    u_ref[s, slot, lt, U_PAD:U_PAD + th, :] = u[:, lt * LANES:(lt + 1) * LANES]
            for lt in range(lane_tiles):
                a = conv(s, slots[0], lt, cols[0] + lt * LANES)
                b = conv(s, slots[1], lt, cols[1] + lt * LANES)
                lo = j * FF_CHUNK + lt * LANES
                act_ref[rows, lo:lo + LANES] = (_silu(a) * b).astype(BF16)

    for rows in groups:
        ffn = jnp.dot(act_ref[rows, :], wd_ref[...], preferred_element_type=F32)
        o_ref[0, rows, :] = o_ref[0, rows, :] + ffn * _rms_scale(ffn) * gain_o


def _out_ffn(x, ret, four, mod, gpost, gpre, gfpost, wo, wup, edge, mid, wd, tm):
    bsz, seq, d = x.shape
    d_ff = wd.shape[0]
    tok = lambda w: pl.BlockSpec((1, tm, w), lambda b, i: (b, i, 0))
    vec = pl.BlockSpec((1, d), lambda b, i: (0, 0))
    const = lambda a: pl.BlockSpec(a.shape, lambda b, i: (0,) * a.ndim, pipeline_mode=pl.Buffered(1))
    return pl.pallas_call(
        _out_ffn_kernel,
        out_shape=jax.ShapeDtypeStruct((bsz, seq, d), F32),
        grid=(bsz, seq // tm),
        in_specs=[tok(d), tok(RET_W), tok(FOURIER_W),
                  pl.BlockSpec((1, 4, d), lambda b, i: (b, 0, 0)), vec, vec, vec,
                  const(wo), const(wup), const(edge), const(mid), const(wd)],
        out_specs=tok(d),
        scratch_shapes=[pltpu.VMEM((tm, d), BF16),
                        pltpu.VMEM((FF_GROUPS, 2 * (d_ff // FF_CHUNK), FF_CHUNK // LANES,
                                    tm // FF_GROUPS + 2 * U_PAD, LANES), F32),
                        pltpu.VMEM((tm, d_ff), BF16)],
        compiler_params=_params("arbitrary", "arbitrary"),
        name="out_ffn",
    )(x, ret, four, mod, gpost, gpre, gfpost, wo, wup, edge, mid, wd)


def _layer(x, ctx, cond, consts, w_ada, b_ada, g_mix_pre, g_mix_post, g_ffn_pre, g_ffn_post,
           w_in, dec_f, dec_b, w_out, w_up, conv_w, conv_b, w_down):
    bsz, seq, d = x.shape
    d_ff = w_down.shape[0]
    row = lambda v: v.reshape(1, -1)

    mod = _adaln(cond, w_ada, row(b_ada))
    mod = mod.reshape(COND_ROWS, N_MOD, d)
    lat = mod[:bsz]
    cmod = mod[bsz]

    decf = jnp.broadcast_to(dec_f[:, None], (RET_HEADS, HEAD_DIM))
    decb = jnp.broadcast_to(dec_b[:, None], (RET_HEADS, HEAD_DIM))
    w_in_bf = w_in.astype(BF16)
    sf0, sb0 = _ctx_states(ctx, cmod[0:1], cmod[1:2], row(g_mix_pre),
                           w_in_bf[:, RET_W:3 * RET_W], decf, decb)

    q, k, v, gate, f = _in_proj(x, lat[:, 0:1], lat[:, 1:2], row(g_mix_pre), w_in_bf,
                                consts["cosf"], consts["sinf"], tm=1024)
    ret = _retention(q, k, v, gate, sf0, sb0, decf, decb)
    four = _fourier(f, consts)

    col = jnp.arange(GRID_W)[:, None]
    edge = jnp.stack([jnp.where(col != 0, conv_w[0][None, :], 0.0),
                      jnp.where(col != GRID_W - 1, conv_w[2][None, :], 0.0)])
    mid = jnp.concatenate([conv_w[1][None], conv_b[None], jnp.zeros((6, 2 * d_ff), F32)], axis=0)
    return _out_ffn(x, ret, four, lat[:, 2:6], row(g_mix_post), row(g_ffn_pre), row(g_ffn_post),
                    w_out.astype(BF16), w_up.astype(BF16), edge, mid, w_down.astype(BF16), tm=512)


def kernel(x, c, ctx, c_ctx, w_ada, b_ada, g_mix_pre, g_mix_post, g_ffn_pre, g_ffn_post, w_in,
           ret_decay_fwd, ret_decay_bwd, w_out, w_up, conv_w, conv_b, w_down):
    bsz, seq, d = x.shape
    depth = w_ada.shape[0]
    assert depth == 1, "the context stream update of deeper stacks is not implemented"
    assert bsz + 1 <= COND_ROWS and seq % (GRID_W * GRID_W) == 0 and seq // GRID_W == GRID_W
    consts = _dft_constants(seq)
    cond = jnp.concatenate([c, c_ctx[None], jnp.zeros((COND_ROWS - bsz - 1, d), F32)], axis=0)
    layer = 0
    return _layer(x, ctx, cond, consts, w_ada[layer], b_ada[layer], g_mix_pre[layer], g_mix_post[layer],
                  g_ffn_pre[layer], g_ffn_post[layer], w_in[layer], ret_decay_fwd[layer],
                  ret_decay_bwd[layer], w_out[layer], w_up[layer], conv_w[layer], conv_b[layer],
                  w_down[layer])
```

```python
import math

import numpy as np
import jax
import jax.numpy as jnp
from jax import lax
from jax.experimental import pallas as pl
from jax.experimental.pallas import tpu as pltpu

F32 = jnp.float32
BF16 = jnp.bfloat16

RET_HEADS = 4
HEAD_DIM = 128
RET_W = RET_HEADS * HEAD_DIM
FOURIER_GROUPS = 4
GROUP_DIM = 128
FOURIER_W = FOURIER_GROUPS * GROUP_DIM
LANES = 128
GRID_W = 64
CHUNK = 128
ROPE_BASE = 10000.0
NORM_EPS = 1e-6
N_MOD = 6
CONV_WIDTH = 3
COND_ROWS = 24
FF_CHUNK = 256
RET_UNROLL = 32
IN_GROUPS = 2
FF_GROUPS = 2
U_PAD = 8
FFT_PITCH = GRID_W + 4
VMEM_LIMIT = 56 * 1024 * 1024


def _silu(x):
    return x / (1.0 + jnp.exp(-x))


def _rms_scale(x):
    return lax.rsqrt(jnp.mean(x * x, axis=-1, keepdims=True) + NORM_EPS)


def _rms_norm(x, g):
    return x * _rms_scale(x) * g


def _log_sigmoid(x):
    return jnp.minimum(x, 0.0) - jnp.log(1.0 + jnp.exp(-jnp.abs(x)))


def _params(*sem):
    return pltpu.CompilerParams(dimension_semantics=sem, vmem_limit_bytes=VMEM_LIMIT)


def _adaln_kernel(cond_ref, w_ref, b_ref, o_ref):
    s = _silu(cond_ref[...])
    o_ref[...] = jnp.dot(s, w_ref[...], preferred_element_type=F32) + b_ref[...]


def _adaln(cond, w, b):
    rows, d = cond.shape
    n = w.shape[1]
    tn = d
    return pl.pallas_call(
        _adaln_kernel,
        out_shape=jax.ShapeDtypeStruct((rows, n), F32),
        grid=(n // tn,),
        in_specs=[pl.BlockSpec((rows, d), lambda j: (0, 0)),
                  pl.BlockSpec((d, tn), lambda j: (0, j)),
                  pl.BlockSpec((1, tn), lambda j: (0, j))],
        out_specs=pl.BlockSpec((rows, tn), lambda j: (0, j)),
        compiler_params=_params("arbitrary"),
        name="adaln",
    )(cond, w, b)


def _ctx_kernel(ctx_ref, sh_ref, sc_ref, g_ref, w_ref, decf_ref, decb_ref, sf_ref, sb_ref):
    n_ctx = ctx_ref.shape[1]
    h = _rms_norm(ctx_ref[0], g_ref[...]) * (1.0 + sc_ref[...]) + sh_ref[...]
    kv = jnp.dot(h.astype(BF16), w_ref[...], preferred_element_type=F32)
    lgf = _log_sigmoid(decf_ref[...])
    lgb = _log_sigmoid(decb_ref[...])
    m = lax.broadcasted_iota(jnp.int32, (n_ctx, HEAD_DIM), 0).astype(F32)
    for hd in range(RET_HEADS):
        k = kv[:, hd * HEAD_DIM:(hd + 1) * HEAD_DIM] * (HEAD_DIM ** -0.5)
        v = kv[:, RET_W + hd * HEAD_DIM:RET_W + (hd + 1) * HEAD_DIM].astype(BF16)
        w_f = jnp.exp((n_ctx - 1.0 - m) * lgf[hd:hd + 1, :])
        w_b = jnp.exp(m * lgb[hd:hd + 1, :])
        dn = (((0,), (0,)), ((), ()))
        sf_ref[0, hd] = lax.dot_general((k * w_f).astype(BF16), v, dn, preferred_element_type=F32)
        sb_ref[0, hd] = lax.dot_general((k * w_b).astype(BF16), v, dn, preferred_element_type=F32)


def _ctx_states(ctx, sh, sc, g, w_kv, decf, decb):
    bsz, n_ctx, d = ctx.shape
    st = jax.ShapeDtypeStruct((bsz, RET_HEADS, HEAD_DIM, HEAD_DIM), F32)
    st_spec = pl.BlockSpec((1, RET_HEADS, HEAD_DIM, HEAD_DIM), lambda b: (b, 0, 0, 0))
    vec = pl.BlockSpec((1, d), lambda b: (0, 0))
    dec = pl.BlockSpec((RET_HEADS, HEAD_DIM), lambda b: (0, 0))
    return pl.pallas_call(
        _ctx_kernel,
        out_shape=(st, st),
        grid=(bsz,),
        in_specs=[pl.BlockSpec((1, n_ctx, d), lambda b: (b, 0, 0)), vec, vec, vec,
                  pl.BlockSpec(w_kv.shape, lambda b: (0, 0)), dec, dec],
        out_specs=(st_spec, st_spec),
        compiler_params=_params("arbitrary"),
        name="ctx_state",
    )(ctx, sh, sc, g, w_kv, decf, decb)


def _in_proj_kernel(x_ref, sh_ref, sc_ref, g_ref, w_ref, cos_ref, sin_ref,
                    q_ref, k_ref, v_ref, gate_ref, f_ref):
    th = x_ref.shape[1] // IN_GROUPS
    gain = g_ref[...] * (1.0 + sc_ref[0])
    for s in range(IN_GROUPS):
        rows = slice(s * th, (s + 1) * th)
        x = x_ref[0, rows, :]
        h = x * _rms_scale(x) * gain + sh_ref[0]
        proj = jnp.dot(h.astype(BF16), w_ref[...], preferred_element_type=F32)
        cosf = cos_ref[rows, :]
        sinf = sin_ref[rows, :]
        for hd in range(RET_HEADS):
            lo = hd * HEAD_DIM
            q = proj[:, lo:lo + HEAD_DIM]
            k = proj[:, RET_W + lo:RET_W + lo + HEAD_DIM] * (HEAD_DIM ** -0.5)
            q_ref[0, hd, rows, :] = (q * cosf + pltpu.roll(q, HEAD_DIM // 2, 1) * sinf).astype(BF16)
            k_ref[0, hd, rows, :] = (k * cosf + pltpu.roll(k, HEAD_DIM // 2, 1) * sinf).astype(BF16)
            v_ref[0, hd, rows, :] = proj[:, 2 * RET_W + lo:2 * RET_W + lo + HEAD_DIM].astype(BF16)
            gate_ref[0, hd, rows, :] = proj[:, 3 * RET_W + lo:3 * RET_W + lo + HEAD_DIM].astype(BF16)
        for g in range(FOURIER_GROUPS):
            lo = 4 * RET_W + g * GROUP_DIM
            f_ref[0, g, rows, :] = proj[:, lo:lo + GROUP_DIM].astype(BF16)


def _in_proj(x, sh, sc, g, w_in, cosf, sinf, tm):
    bsz, seq, d = x.shape
    mod = pl.BlockSpec((1, 1, d), lambda b, i: (b, 0, 0))
    split = jax.ShapeDtypeStruct((bsz, RET_HEADS, seq, HEAD_DIM), BF16)
    split_spec = pl.BlockSpec((1, RET_HEADS, tm, HEAD_DIM), lambda b, i: (b, 0, i, 0))
    rope = pl.BlockSpec((tm, HEAD_DIM), lambda b, i: (i, 0))
    return pl.pallas_call(
        _in_proj_kernel,
        out_shape=(split,) * 5,
        grid=(bsz, seq // tm),
        in_specs=[pl.BlockSpec((1, tm, d), lambda b, i: (b, i, 0)), mod, mod,
                  pl.BlockSpec((1, d), lambda b, i: (0, 0)),
                  pl.BlockSpec(w_in.shape, lambda b, i: (0, 0)), rope, rope],
        out_specs=(split_spec,) * 5,
        compiler_params=_params("arbitrary", "arbitrary"),
        name="in_proj",
    )(x, sh, sc, g, w_in, cosf, sinf)


def _retention_kernel(q_ref, k_ref, v_ref, gate_ref, sf0_ref, sb0_ref, decf_ref, decb_ref,
                      o_ref, kvf_scr, kvb_scr, sf_scr, sb_scr):
    hd = pl.program_id(1)
    n_chunks = q_ref.shape[2] // CHUNK
    lgf = _log_sigmoid(decf_ref[pl.ds(hd, 1), :])
    lgb = _log_sigmoid(decb_ref[pl.ds(hd, 1), :])
    pos = lax.broadcasted_iota(jnp.int32, (CHUNK, HEAD_DIM), 0).astype(F32)
    col = lax.broadcasted_iota(jnp.int32, (CHUNK, CHUNK), 1).astype(F32)
    zeta_f = jnp.exp((CHUNK - 1.0 - pos) * lgf)
    zeta_b = jnp.exp(pos * lgb)
    xi_f = jnp.exp((pos + 1.0) * lgf)
    xi_b = jnp.exp((CHUNK - pos) * lgb)
    diff = pos - col
    decay = jnp.exp(jnp.where(diff >= 0, diff * lgf, -diff * lgb))
    cdec_f = jnp.exp(CHUNK * lgf)
    cdec_b = jnp.exp(CHUNK * lgb)
    tn = (((0,), (0,)), ((), ()))
    nt = (((1,), (1,)), ((), ()))

    def chunk_kv(n, carry):
        rows = pl.ds(pl.multiple_of(n * CHUNK, CHUNK), CHUNK)
        k = k_ref[0, 0, rows, :].astype(F32)
        v = v_ref[0, 0, rows, :]
        kvf_scr[n] = lax.dot_general((k * zeta_f).astype(BF16), v, tn, preferred_element_type=F32)
        kvb_scr[n] = lax.dot_general((k * zeta_b).astype(BF16), v, tn, preferred_element_type=F32)
        return carry

    lax.fori_loop(0, n_chunks, chunk_kv, 0, unroll=RET_UNROLL)

    def scan_f(n, state):
        sf_scr[n] = state.astype(BF16)
        return state * cdec_f + kvf_scr[n]

    lax.fori_loop(0, n_chunks, scan_f, sf0_ref[0, 0])

    def scan_b(i, state):
        n = n_chunks - 1 - i
        sb_scr[n] = state.astype(BF16)
        return state * cdec_b + kvb_scr[n]

    lax.fori_loop(0, n_chunks, scan_b, sb0_ref[0, 0])

    def chunk_out(n, carry):
        rows = pl.ds(pl.multiple_of(n * CHUNK, CHUNK), CHUNK)
        q = q_ref[0, 0, rows, :]
        k = k_ref[0, 0, rows, :]
        v = v_ref[0, 0, rows, :]
        scores = lax.dot_general(q, k, nt, preferred_element_type=F32) * decay
        qf = q.astype(F32)
        o = jnp.dot(scores.astype(BF16), v, preferred_element_type=F32)
        o = o + jnp.dot((qf * xi_f).astype(BF16), sf_scr[n], preferred_element_type=F32)
        o = o + jnp.dot((qf * xi_b).astype(BF16), sb_scr[n], preferred_element_type=F32)
        o = o * lax.rsqrt(jnp.mean(o * o, axis=-1, keepdims=True) + NORM_EPS)
        o_ref[0, 0, rows, :] = (o * _silu(gate_ref[0, 0, rows, :].astype(F32))).astype(BF16)
        return carry

    lax.fori_loop(0, n_chunks, chunk_out, 0, unroll=RET_UNROLL)


def _retention(q, k, v, gate, sf0, sb0, decf, decb):
    bsz, _, seq, _ = q.shape
    n_chunks = seq // CHUNK
    tok = pl.BlockSpec((1, 1, seq, HEAD_DIM), lambda b, h: (b, h, 0, 0))
    st = pl.BlockSpec((1, 1, HEAD_DIM, HEAD_DIM), lambda b, h: (b, h, 0, 0))
    dec = pl.BlockSpec((RET_HEADS, HEAD_DIM), lambda b, h: (0, 0))
    return pl.pallas_call(
        _retention_kernel,
        out_shape=jax.ShapeDtypeStruct(q.shape, BF16),
        grid=(bsz, RET_HEADS),
        in_specs=[tok, tok, tok, tok, st, st, dec, dec],
        out_specs=tok,
        scratch_shapes=[pltpu.VMEM((n_chunks, HEAD_DIM, HEAD_DIM), F32),
                        pltpu.VMEM((n_chunks, HEAD_DIM, HEAD_DIM), F32),
                        pltpu.VMEM((n_chunks, HEAD_DIM, HEAD_DIM), BF16),
                        pltpu.VMEM((n_chunks, HEAD_DIM, HEAD_DIM), BF16)],
        compiler_params=_params("arbitrary", "arbitrary"),
        name="retention",
    )(q, k, v, gate, sf0, sb0, decf, decb)


def _fourier_kernel(f_ref, w1_ref, twc_ref, tws_ref, w2_ref, wc_ref, o_ref,
                    xs, ar_s, ai_s, xr_s, xi_s, lhs_s):
    gh = xs.shape[0] // FFT_PITCH
    for t1 in range(gh):
        xs[pl.ds(t1 * FFT_PITCH, GRID_W), :] = f_ref[0, 0, pl.ds(t1 * GRID_W, GRID_W), :].astype(F32)
    for t2 in range(0, GRID_W, 2):
        x = jnp.concatenate([xs[pl.ds(t2 + u, gh, stride=FFT_PITCH), :] for u in range(2)], axis=1)
        r = jnp.dot(w1_ref[...], x.astype(BF16), preferred_element_type=F32)
        for u in range(2):
            lanes = slice(u * GROUP_DIM, (u + 1) * GROUP_DIM)
            ar_s[pl.ds(t2 + u, gh, stride=FFT_PITCH), :] = r[:gh, lanes]
            ai_s[pl.ds(t2 + u, gh, stride=FFT_PITCH), :] = r[gh:, lanes]
    for k1 in range(0, gh, 2):
        cols = []
        for u in range(2):
            rows = pl.ds((k1 + u) * FFT_PITCH, GRID_W)
            tw_rows = pl.ds((k1 + u) * GRID_W, GRID_W)
            ar, ai = ar_s[rows, :], ai_s[rows, :]
            twc, tws = twc_ref[tw_rows, :], tws_ref[tw_rows, :]
            cols.append(jnp.concatenate([ar * twc + ai * tws, ai * twc - ar * tws], axis=0))
        r = jnp.dot(w2_ref[...], jnp.concatenate(cols, axis=1).astype(BF16), preferred_element_type=F32)
        for u in range(2):
            lanes = slice(u * GROUP_DIM, (u + 1) * GROUP_DIM)
            xr_s[pl.ds(k1 + u, GRID_W, stride=FFT_PITCH), :] = r[:GRID_W, lanes]
            xi_s[pl.ds(k1 + u, GRID_W, stride=FFT_PITCH), :] = r[GRID_W:, lanes]
    for k2 in range(GRID_W):
        rows = pl.ds(k2 * FFT_PITCH, gh)
        lhs_s[pl.ds(k2 * gh, gh), :GROUP_DIM] = xr_s[rows, :].astype(BF16)
        lhs_s[pl.ds(k2 * gh, gh), GROUP_DIM:] = xi_s[rows, :].astype(BF16)
    o_ref[0, 0] = jnp.dot(lhs_s[...], wc_ref[...], preferred_element_type=F32).astype(BF16)


def _fourier(f, consts):
    bsz, n_groups, seq, _ = f.shape
    gh = seq // GRID_W
    tok = pl.BlockSpec((1, 1, seq, GROUP_DIM), lambda b, g: (b, g, 0, 0))
    const = lambda a: pl.BlockSpec(a.shape, lambda b, g: (0,) * a.ndim)
    names = ("w1", "twc", "tws", "w2", "wc")
    pitched = pltpu.VMEM((gh * FFT_PITCH, GROUP_DIM), F32)
    return pl.pallas_call(
        _fourier_kernel,
        out_shape=jax.ShapeDtypeStruct(f.shape, BF16),
        grid=(bsz, n_groups),
        in_specs=[tok] + [const(consts[n]) for n in names],
        out_specs=tok,
        scratch_shapes=[pitched] * 5 + [pltpu.VMEM((seq, 2 * GROUP_DIM), BF16)],
        compiler_params=_params("arbitrary", "arbitrary"),
        name="fourier",
    )(f, *[consts[n] for n in names])


def _dft_constants(seq):
    gh = seq // GRID_W
    def cs(n_rows, n_cols, period):
        ang = 2.0 * np.pi * np.outer(np.arange(n_rows), np.arange(n_cols)) / period
        return np.cos(ang), np.sin(ang)
    c1, s1 = cs(gh, gh, gh)
    w1 = np.concatenate([c1, -s1], axis=0) / gh
    c2, s2 = cs(GRID_W, GRID_W, GRID_W)
    w2 = np.block([[c2, s2], [-s2, c2]]) / math.sqrt(GRID_W * gh) * gh
    cc, sc = cs(GROUP_DIM, GROUP_DIM, GROUP_DIM)
    wc = np.concatenate([cc, sc], axis=0) / math.sqrt(GROUP_DIM)
    k1 = np.repeat(np.arange(gh), GRID_W)
    t2 = np.tile(np.arange(GRID_W), gh)
    ang = 2.0 * np.pi * (k1 * t2 % seq) / seq
    twc = np.broadcast_to(np.cos(ang)[:, None], (seq, GROUP_DIM))
    tws = np.broadcast_to(np.sin(ang)[:, None], (seq, GROUP_DIM))
    t = np.arange(seq)
    n_freq = HEAD_DIM // 4
    freqs = ROPE_BASE ** (-np.arange(n_freq, dtype=np.float32) / n_freq)
    rang = np.concatenate([(t // GRID_W).astype(np.float32)[:, None] * freqs,
                           (t % GRID_W).astype(np.float32)[:, None] * freqs], axis=-1).astype(np.float32)
    cosf = np.concatenate([np.cos(rang), np.cos(rang)], axis=-1)
    sinf = np.concatenate([-np.sin(rang), np.sin(rang)], axis=-1)
    as_f32 = lambda a: jnp.asarray(np.ascontiguousarray(a), F32)
    as_bf16 = lambda a: jnp.asarray(np.ascontiguousarray(a), F32).astype(BF16)
    return dict(w1=as_bf16(w1), w2=as_bf16(w2), wc=as_bf16(wc), twc=as_f32(twc), tws=as_f32(tws),
                cosf=as_f32(cosf), sinf=as_f32(sinf))


def _out_ffn_kernel(x_ref, ret_ref, four_ref, mod_ref, gpost_ref, gpre_ref, gfpost_ref,
                    wo_ref, wup_ref, taps_ref, wd_ref, o_ref, h_ref, u_ref, act_ref):
    n_groups = u_ref.shape[0]
    th = x_ref.shape[1] // n_groups
    d_ff = wd_ref.shape[0]
    gt_a, sh_f, sc_f, gt_f = mod_ref[0, 0:1], mod_ref[0, 1:2], mod_ref[0, 2:3], mod_ref[0, 3:4]
    groups = [slice(s * th, (s + 1) * th) for s in range(n_groups)]
    gain_a = gpost_ref[...] * gt_a
    gain_f = gpre_ref[...] * (1.0 + sc_f)
    gain_o = gfpost_ref[...] * gt_f

    for s, rows in enumerate(groups):
        mixed = [ret_ref[0, i, rows, :] for i in range(RET_HEADS)]
        mixed += [four_ref[0, i, rows, :] for i in range(FOURIER_GROUPS)]
        mix = jnp.dot(jnp.concatenate(mixed, axis=1), wo_ref[...], preferred_element_type=F32)
        x1 = x_ref[0, rows, :] + mix * _rms_scale(mix) * gain_a
        o_ref[0, rows, :] = x1
        h_ref[rows, :] = (x1 * _rms_scale(x1) * gain_f + sh_f).astype(BF16)
        u_ref[s, :, :, 0:U_PAD, :] = jnp.zeros(u_ref.shape[1:3] + (U_PAD, LANES), F32)
        u_ref[s, :, :, U_PAD + th:, :] = jnp.zeros(u_ref.shape[1:3] + (U_PAD, LANES), F32)

    lane_tiles = FF_CHUNK // LANES

    sub = lax.broadcasted_iota(jnp.int32, (8, LANES), 0)
    not_first = (sub != 0).astype(F32)
    not_last = (sub != 7).astype(F32)

    def conv(s, slot, lt, lo):
        w_prev, w_cur, w_next, bias = (taps_ref[i:i + 1, lo:lo + LANES] for i in range(4))
        blocks = []
        for r in range(0, th, GRID_W):
            prev = u_ref[s, slot, lt, U_PAD - 1 + r:U_PAD - 1 + r + GRID_W, :]
            nxt = u_ref[s, slot, lt, U_PAD + 1 + r:U_PAD + 1 + r + GRID_W, :]
            cur = u_ref[s, slot, lt, U_PAD + r:U_PAD + r + GRID_W, :]
            prev = jnp.concatenate([prev[:8] * not_first, prev[8:]], axis=0)
            nxt = jnp.concatenate([nxt[:GRID_W - 8], nxt[GRID_W - 8:] * not_last], axis=0)
            blocks.append(prev * w_prev + nxt * w_next + cur * w_cur + bias)
        return jnp.concatenate(blocks, axis=0)

    for j in range(d_ff // FF_CHUNK):
        cols = (j * FF_CHUNK, d_ff + j * FF_CHUNK)
        slots = (2 * j, 2 * j + 1)
        for s, rows in enumerate(groups):
            h = h_ref[rows, :]
            for slot, lo in zip(slots, cols):
                u = jnp.dot(h, wup_ref[:, lo:lo + FF_CHUNK], preferred_element_type=F32)
                for lt in range(lane_tiles):
                    u_ref[s, slot, lt, U_PAD:U_PAD + th, :] = u[:, lt * LANES:(lt + 1) * LANES]
            for lt in range(lane_tiles):
                a = conv(s, slots[0], lt, cols[0] + lt * LANES)
                b = conv(s, slots[1], lt, cols[1] + lt * LANES)
                lo = j * FF_CHUNK + lt * LANES
                act_ref[rows, lo:lo + LANES] = (_silu(a) * b).astype(BF16)

    for rows in groups:
        ffn = jnp.dot(act_ref[rows, :], wd_ref[...], preferred_element_type=F32)
        o_ref[0, rows, :] = o_ref[0, rows, :] + ffn * _rms_scale(ffn) * gain_o


def _out_ffn(x, ret, four, mod, gpost, gpre, gfpost, wo, wup, taps, wd, tm):
    bsz, seq, d = x.shape
    d_ff = wd.shape[0]
    tok = lambda w: pl.BlockSpec((1, tm, w), lambda b, i: (b, i, 0))
    split = lambda n: pl.BlockSpec((1, n, tm, LANES), lambda b, i: (b, 0, i, 0))
    vec = pl.BlockSpec((1, d), lambda b, i: (0, 0))
    const = lambda a: pl.BlockSpec(a.shape, lambda b, i: (0,) * a.ndim, pipeline_mode=pl.Buffered(1))
    return pl.pallas_call(
        _out_ffn_kernel,
        out_shape=jax.ShapeDtypeStruct((bsz, seq, d), F32),
        grid=(bsz, seq // tm),
        in_specs=[tok(d), split(RET_HEADS), split(FOURIER_GROUPS),
                  pl.BlockSpec((1, 4, d), lambda b, i: (b, 0, 0)), vec, vec, vec,
                  const(wo), const(wup), const(taps), const(wd)],
        out_specs=tok(d),
        scratch_shapes=[pltpu.VMEM((tm, d), BF16),
                        pltpu.VMEM((FF_GROUPS, 2 * (d_ff // FF_CHUNK), FF_CHUNK // LANES,
                                    tm // FF_GROUPS + 2 * U_PAD, LANES), F32),
                        pltpu.VMEM((tm, d_ff), BF16)],
        compiler_params=_params("arbitrary", "arbitrary"),
        name="out_ffn",
    )(x, ret, four, mod, gpost, gpre, gfpost, wo, wup, taps, wd)


def _layer(x, ctx, cond, consts, w_ada, b_ada, g_mix_pre, g_mix_post, g_ffn_pre, g_ffn_post,
           w_in, dec_f, dec_b, w_out, w_up, conv_w, conv_b, w_down):
    bsz, seq, d = x.shape
    d_ff = w_down.shape[0]
    row = lambda v: v.reshape(1, -1)

    mod = _adaln(cond, w_ada, row(b_ada))
    mod = mod.reshape(COND_ROWS, N_MOD, d)
    lat = mod[:bsz]
    cmod = mod[bsz]

    decf = jnp.broadcast_to(dec_f[:, None], (RET_HEADS, HEAD_DIM))
    decb = jnp.broadcast_to(dec_b[:, None], (RET_HEADS, HEAD_DIM))
    w_in_bf = w_in.astype(BF16)
    sf0, sb0 = _ctx_states(ctx, cmod[0:1], cmod[1:2], row(g_mix_pre),
                           w_in_bf[:, RET_W:3 * RET_W], decf, decb)

    q, k, v, gate, f = _in_proj(x, lat[:, 0:1], lat[:, 1:2], row(g_mix_pre), w_in_bf,
                                consts["cosf"], consts["sinf"], tm=1024)
    ret = _retention(q, k, v, gate, sf0, sb0, decf, decb)
    four = _fourier(f, consts)

    taps = jnp.concatenate([conv_w, conv_b[None], jnp.zeros((8 - CONV_WIDTH - 1, 2 * d_ff), F32)], axis=0)
    return _out_ffn(x, ret, four, lat[:, 2:6], row(g_mix_post), row(g_ffn_pre), row(g_ffn_post),
                    w_out.astype(BF16), w_up.astype(BF16), taps, w_down.astype(BF16), tm=512)


def kernel(x, c, ctx, c_ctx, w_ada, b_ada, g_mix_pre, g_mix_post, g_ffn_pre, g_ffn_post, w_in,
           ret_decay_fwd, ret_decay_bwd, w_out, w_up, conv_w, conv_b, w_down):
    bsz, seq, d = x.shape
    depth = w_ada.shape[0]
    assert depth == 1, "the context stream update of deeper stacks is not implemented"
    assert bsz + 1 <= COND_ROWS and seq % (GRID_W * GRID_W) == 0 and seq // GRID_W == GRID_W
    consts = _dft_constants(seq)
    cond = jnp.concatenate([c, c_ctx[None], jnp.zeros((COND_ROWS - bsz - 1, d), F32)], axis=0)
    layer = 0
    return _layer(x, ctx, cond, consts, w_ada[layer], b_ada[layer], g_mix_pre[layer], g_mix_post[layer],
                  g_ffn_pre[layer], g_ffn_post[layer], w_in[layer], ret_decay_fwd[layer],
                  ret_decay_bwd[layer], w_out[layer], w_up[layer], conv_w[layer], conv_b[layer],
                  w_down[layer])
```

```python
import math

import numpy as np
import jax
import jax.numpy as jnp
from jax import lax
from jax.experimental import pallas as pl
from jax.experimental.pallas import tpu as pltpu

F32 = jnp.float32
BF16 = jnp.bfloat16

RET_HEADS = 4
HEAD_DIM = 128
RET_W = RET_HEADS * HEAD_DIM
FOURIER_GROUPS = 4
GROUP_DIM = 128
FOURIER_W = FOURIER_GROUPS * GROUP_DIM
LANES = 128
GRID_W = 64
CHUNK = 256
ROPE_BASE = 10000.0
NORM_EPS = 1e-6
N_MOD = 6
CONV_WIDTH = 3
COND_ROWS = 24
FF_CHUNK = 256
RET_UNROLL = 16
IN_GROUPS = 2
FF_GROUPS = 2
U_PAD = 8
FFT_PITCH = GRID_W + 4
VMEM_LIMIT = 56 * 1024 * 1024


def _silu(x):
    return x / (1.0 + jnp.exp(-x))


def _rms_scale(x):
    return lax.rsqrt(jnp.mean(x * x, axis=-1, keepdims=True) + NORM_EPS)


def _rms_norm(x, g):
    return x * _rms_scale(x) * g


def _log_sigmoid(x):
    return jnp.minimum(x, 0.0) - jnp.log(1.0 + jnp.exp(-jnp.abs(x)))


def _params(*sem):
    return pltpu.CompilerParams(dimension_semantics=sem, vmem_limit_bytes=VMEM_LIMIT)


def _adaln_kernel(cond_ref, w_ref, b_ref, o_ref):
    s = _silu(cond_ref[...])
    o_ref[...] = jnp.dot(s, w_ref[...], preferred_element_type=F32) + b_ref[...]


def _adaln(cond, w, b):
    rows, d = cond.shape
    n = w.shape[1]
    tn = d
    return pl.pallas_call(
        _adaln_kernel,
        out_shape=jax.ShapeDtypeStruct((rows, n), F32),
        grid=(n // tn,),
        in_specs=[pl.BlockSpec((rows, d), lambda j: (0, 0)),
                  pl.BlockSpec((d, tn), lambda j: (0, j)),
                  pl.BlockSpec((1, tn), lambda j: (0, j))],
        out_specs=pl.BlockSpec((rows, tn), lambda j: (0, j)),
        compiler_params=_params("arbitrary"),
        name="adaln",
    )(cond, w, b)


def _ctx_kernel(ctx_ref, sh_ref, sc_ref, g_ref, w_ref, decf_ref, decb_ref, sf_ref, sb_ref):
    n_ctx = ctx_ref.shape[1]
    h = _rms_norm(ctx_ref[0], g_ref[...]) * (1.0 + sc_ref[...]) + sh_ref[...]
    kv = jnp.dot(h.astype(BF16), w_ref[...], preferred_element_type=F32)
    lgf = _log_sigmoid(decf_ref[...])
    lgb = _log_sigmoid(decb_ref[...])
    m = lax.broadcasted_iota(jnp.int32, (n_ctx, HEAD_DIM), 0).astype(F32)
    for hd in range(RET_HEADS):
        k = kv[:, hd * HEAD_DIM:(hd + 1) * HEAD_DIM] * (HEAD_DIM ** -0.5)
        v = kv[:, RET_W + hd * HEAD_DIM:RET_W + (hd + 1) * HEAD_DIM].astype(BF16)
        w_f = jnp.exp((n_ctx - 1.0 - m) * lgf[hd:hd + 1, :])
        w_b = jnp.exp(m * lgb[hd:hd + 1, :])
        dn = (((0,), (0,)), ((), ()))
        sf_ref[0, hd] = lax.dot_general((k * w_f).astype(BF16), v, dn, preferred_element_type=F32)
        sb_ref[0, hd] = lax.dot_general((k * w_b).astype(BF16), v, dn, preferred_element_type=F32)


def _ctx_states(ctx, sh, sc, g, w_kv, decf, decb):
    bsz, n_ctx, d = ctx.shape
    st = jax.ShapeDtypeStruct((bsz, RET_HEADS, HEAD_DIM, HEAD_DIM), F32)
    st_spec = pl.BlockSpec((1, RET_HEADS, HEAD_DIM, HEAD_DIM), lambda b: (b, 0, 0, 0))
    vec = pl.BlockSpec((1, d), lambda b: (0, 0))
    dec = pl.BlockSpec((RET_HEADS, HEAD_DIM), lambda b: (0, 0))
    return pl.pallas_call(
        _ctx_kernel,
        out_shape=(st, st),
        grid=(bsz,),
        in_specs=[pl.BlockSpec((1, n_ctx, d), lambda b: (b, 0, 0)), vec, vec, vec,
                  pl.BlockSpec(w_kv.shape, lambda b: (0, 0)), dec, dec],
        out_specs=(st_spec, st_spec),
        compiler_params=_params("arbitrary"),
        name="ctx_state",
    )(ctx, sh, sc, g, w_kv, decf, decb)


def _in_proj_kernel(x_ref, sh_ref, sc_ref, g_ref, w_ref, cos_ref, sin_ref,
                    q_ref, k_ref, v_ref, gate_ref, f_ref):
    th = x_ref.shape[1] // IN_GROUPS
    gain = g_ref[...] * (1.0 + sc_ref[0])
    for s in range(IN_GROUPS):
        rows = slice(s * th, (s + 1) * th)
        x = x_ref[0, rows, :]
        h = x * _rms_scale(x) * gain + sh_ref[0]
        proj = jnp.dot(h.astype(BF16), w_ref[...], preferred_element_type=F32)
        cosf = cos_ref[rows, :]
        sinf = sin_ref[rows, :]
        for hd in range(RET_HEADS):
            lo = hd * HEAD_DIM
            q = proj[:, lo:lo + HEAD_DIM]
            k = proj[:, RET_W + lo:RET_W + lo + HEAD_DIM] * (HEAD_DIM ** -0.5)
            q_ref[0, hd, rows, :] = (q * cosf + pltpu.roll(q, HEAD_DIM // 2, 1) * sinf).astype(BF16)
            k_ref[0, hd, rows, :] = (k * cosf + pltpu.roll(k, HEAD_DIM // 2, 1) * sinf).astype(BF16)
            v_ref[0, hd, rows, :] = proj[:, 2 * RET_W + lo:2 * RET_W + lo + HEAD_DIM].astype(BF16)
            gate_ref[0, hd, rows, :] = proj[:, 3 * RET_W + lo:3 * RET_W + lo + HEAD_DIM].astype(BF16)
        for g in range(FOURIER_GROUPS):
            lo = 4 * RET_W + g * GROUP_DIM
            f_ref[0, g, rows, :] = proj[:, lo:lo + GROUP_DIM].astype(BF16)


def _in_proj(x, sh, sc, g, w_in, cosf, sinf, tm):
    bsz, seq, d = x.shape
    mod = pl.BlockSpec((1, 1, d), lambda b, i: (b, 0, 0))
    split = jax.ShapeDtypeStruct((bsz, RET_HEADS, seq, HEAD_DIM), BF16)
    split_spec = pl.BlockSpec((1, RET_HEADS, tm, HEAD_DIM), lambda b, i: (b, 0, i, 0))
    rope = pl.BlockSpec((tm, HEAD_DIM), lambda b, i: (i, 0))
    return pl.pallas_call(
        _in_proj_kernel,
        out_shape=(split,) * 5,
        grid=(bsz, seq // tm),
        in_specs=[pl.BlockSpec((1, tm, d), lambda b, i: (b, i, 0)), mod, mod,
                  pl.BlockSpec((1, d), lambda b, i: (0, 0)),
                  pl.BlockSpec(w_in.shape, lambda b, i: (0, 0)), rope, rope],
        out_specs=(split_spec,) * 5,
        compiler_params=_params("arbitrary", "arbitrary"),
        name="in_proj",
    )(x, sh, sc, g, w_in, cosf, sinf)


def _retention_kernel(q_ref, k_ref, v_ref, gate_ref, sf0_ref, sb0_ref, decf_ref, decb_ref,
                      o_ref, kvf_scr, kvb_scr, sf_scr, sb_scr):
    hd = pl.program_id(1)
    n_chunks = q_ref.shape[2] // CHUNK
    lgf = _log_sigmoid(decf_ref[pl.ds(hd, 1), :])
    lgb = _log_sigmoid(decb_ref[pl.ds(hd, 1), :])
    pos = lax.broadcasted_iota(jnp.int32, (CHUNK, HEAD_DIM), 0).astype(F32)
    diff = (lax.broadcasted_iota(jnp.int32, (CHUNK, CHUNK), 0)
            - lax.broadcasted_iota(jnp.int32, (CHUNK, CHUNK), 1)).astype(F32)
    lgf_c = jnp.concatenate([lgf] * (CHUNK // HEAD_DIM), axis=1)
    lgb_c = jnp.concatenate([lgb] * (CHUNK // HEAD_DIM), axis=1)
    zeta_f = jnp.exp((CHUNK - 1.0 - pos) * lgf)
    zeta_b = jnp.exp(pos * lgb)
    xi_f = jnp.exp((pos + 1.0) * lgf)
    xi_b = jnp.exp((CHUNK - pos) * lgb)
    decay = jnp.exp(jnp.where(diff >= 0, diff * lgf_c, -diff * lgb_c))
    cdec_f = jnp.exp(CHUNK * lgf)
    cdec_b = jnp.exp(CHUNK * lgb)
    tn = (((0,), (0,)), ((), ()))
    nt = (((1,), (1,)), ((), ()))

    def chunk_kv(n, carry):
        rows = pl.ds(pl.multiple_of(n * CHUNK, CHUNK), CHUNK)
        k = k_ref[0, 0, rows, :].astype(F32)
        v = v_ref[0, 0, rows, :]
        kvf_scr[n] = lax.dot_general((k * zeta_f).astype(BF16), v, tn, preferred_element_type=F32)
        kvb_scr[n] = lax.dot_general((k * zeta_b).astype(BF16), v, tn, preferred_element_type=F32)
        return carry

    lax.fori_loop(0, n_chunks, chunk_kv, 0, unroll=RET_UNROLL)

    def scan_f(n, state):
        sf_scr[n] = state.astype(BF16)
        return state * cdec_f + kvf_scr[n]

    lax.fori_loop(0, n_chunks, scan_f, sf0_ref[0, 0])

    def scan_b(i, state):
        n = n_chunks - 1 - i
        sb_scr[n] = state.astype(BF16)
        return state * cdec_b + kvb_scr[n]

    lax.fori_loop(0, n_chunks, scan_b, sb0_ref[0, 0])

    def chunk_out(n, carry):
        rows = pl.ds(pl.multiple_of(n * CHUNK, CHUNK), CHUNK)
        q = q_ref[0, 0, rows, :]
        k = k_ref[0, 0, rows, :]
        v = v_ref[0, 0, rows, :]
        scores = lax.dot_general(q, k, nt, preferred_element_type=F32) * decay
        qf = q.astype(F32)
        o = jnp.dot(scores.astype(BF16), v, preferred_element_type=F32)
        o = o + jnp.dot((qf * xi_f).astype(BF16), sf_scr[n], preferred_element_type=F32)
        o = o + jnp.dot((qf * xi_b).astype(BF16), sb_scr[n], preferred_element_type=F32)
        o = o * lax.rsqrt(jnp.mean(o * o, axis=-1, keepdims=True) + NORM_EPS)
        o_ref[0, 0, rows, :] = (o * _silu(gate_ref[0, 0, rows, :].astype(F32))).astype(BF16)
        return carry

    lax.fori_loop(0, n_chunks, chunk_out, 0, unroll=RET_UNROLL)


def _retention(q, k, v, gate, sf0, sb0, decf, decb):
    bsz, _, seq, _ = q.shape
    n_chunks = seq // CHUNK
    tok = pl.BlockSpec((1, 1, seq, HEAD_DIM), lambda b, h: (b, h, 0, 0))
    st = pl.BlockSpec((1, 1, HEAD_DIM, HEAD_DIM), lambda b, h: (b, h, 0, 0))
    dec = pl.BlockSpec((RET_HEADS, HEAD_DIM), lambda b, h: (0, 0))
    return pl.pallas_call(
        _retention_kernel,
        out_shape=jax.ShapeDtypeStruct(q.shape, BF16),
        grid=(bsz, RET_HEADS),
        in_specs=[tok, tok, tok, tok, st, st, dec, dec],
        out_specs=tok,
        scratch_shapes=[pltpu.VMEM((n_chunks, HEAD_DIM, HEAD_DIM), F32),
                        pltpu.VMEM((n_chunks, HEAD_DIM, HEAD_DIM), F32),
                        pltpu.VMEM((n_chunks, HEAD_DIM, HEAD_DIM), BF16),
                        pltpu.VMEM((n_chunks, HEAD_DIM, HEAD_DIM), BF16)],
        compiler_params=_params("arbitrary", "arbitrary"),
        name="retention",
    )(q, k, v, gate, sf0, sb0, decf, decb)


def _fourier_kernel(f_ref, w1_ref, w2_ref, wc_ref, o_ref,
                    xs, ar_s, ai_s, xr_s, xi_s, lhs_s):
    gh = xs.shape[0] // FFT_PITCH
    for t1 in range(gh):
        xs[pl.ds(t1 * FFT_PITCH, GRID_W), :] = f_ref[0, 0, pl.ds(t1 * GRID_W, GRID_W), :].astype(F32)
    for t2 in range(0, GRID_W, 2):
        x = jnp.concatenate([xs[pl.ds(t2 + u, gh, stride=FFT_PITCH), :] for u in range(2)], axis=1)
        r = jnp.dot(w1_ref[...], x.astype(BF16), preferred_element_type=F32)
        for u in range(2):
            lanes = slice(u * GROUP_DIM, (u + 1) * GROUP_DIM)
            ar_s[pl.ds(t2 + u, gh, stride=FFT_PITCH), :] = r[:gh, lanes]
            ai_s[pl.ds(t2 + u, gh, stride=FFT_PITCH), :] = r[gh:, lanes]
    for k1 in range(gh):
        rows = pl.ds(k1 * FFT_PITCH, GRID_W)
        stacked = jnp.concatenate([ar_s[rows, :], ai_s[rows, :]], axis=0).astype(BF16)
        r = jnp.dot(w2_ref[k1], stacked, preferred_element_type=F32)
        xr_s[pl.ds(k1, GRID_W, stride=FFT_PITCH), :] = r[:GRID_W]
        xi_s[pl.ds(k1, GRID_W, stride=FFT_PITCH), :] = r[GRID_W:]
    for k2 in range(GRID_W):
        rows = pl.ds(k2 * FFT_PITCH, gh)
        lhs_s[pl.ds(k2 * gh, gh), :GROUP_DIM] = xr_s[rows, :].astype(BF16)
        lhs_s[pl.ds(k2 * gh, gh), GROUP_DIM:] = xi_s[rows, :].astype(BF16)
    o_ref[0, 0] = jnp.dot(lhs_s[...], wc_ref[...], preferred_element_type=F32).astype(BF16)


def _fourier(f, consts):
    bsz, n_groups, seq, _ = f.shape
    gh = seq // GRID_W
    tok = pl.BlockSpec((1, 1, seq, GROUP_DIM), lambda b, g: (b, g, 0, 0))
    const = lambda a: pl.BlockSpec(a.shape, lambda b, g: (0,) * a.ndim)
    names = ("w1", "w2", "wc")
    pitched = pltpu.VMEM((gh * FFT_PITCH, GROUP_DIM), F32)
    return pl.pallas_call(
        _fourier_kernel,
        out_shape=jax.ShapeDtypeStruct(f.shape, BF16),
        grid=(bsz, n_groups),
        in_specs=[tok] + [const(consts[n]) for n in names],
        out_specs=tok,
        scratch_shapes=[pitched] * 5 + [pltpu.VMEM((seq, 2 * GROUP_DIM), BF16)],
        compiler_params=_params("arbitrary", "arbitrary"),
        name="fourier",
    )(f, *[consts[n] for n in names])


def _dft_constants(seq):
    gh = seq // GRID_W
    def cs(n_rows, n_cols, period):
        ang = 2.0 * np.pi * np.outer(np.arange(n_rows), np.arange(n_cols)) / period
        return np.cos(ang), np.sin(ang)
    c1, s1 = cs(gh, gh, gh)
    w1 = np.concatenate([c1, -s1], axis=0) / gh
    c2, s2 = cs(GRID_W, GRID_W, GRID_W)
    cc, sc = cs(GROUP_DIM, GROUP_DIM, GROUP_DIM)
    wc = np.concatenate([cc, sc], axis=0) / math.sqrt(GROUP_DIM)
    twc, tws = cs(gh, GRID_W, seq)
    p = c2[None] * twc[:, None, :] - s2[None] * tws[:, None, :]
    q = s2[None] * twc[:, None, :] + c2[None] * tws[:, None, :]
    w2 = np.concatenate([np.concatenate([p, q], axis=2), np.concatenate([-q, p], axis=2)], axis=1)
    w2 = w2 / math.sqrt(GRID_W * gh) * gh
    t = np.arange(seq)
    n_freq = HEAD_DIM // 4
    freqs = ROPE_BASE ** (-np.arange(n_freq, dtype=np.float32) / n_freq)
    rang = np.concatenate([(t // GRID_W).astype(np.float32)[:, None] * freqs,
                           (t % GRID_W).astype(np.float32)[:, None] * freqs], axis=-1).astype(np.float32)
    cosf = np.concatenate([np.cos(rang), np.cos(rang)], axis=-1)
    sinf = np.concatenate([-np.sin(rang), np.sin(rang)], axis=-1)
    as_f32 = lambda a: jnp.asarray(np.ascontiguousarray(a), F32)
    as_bf16 = lambda a: jnp.asarray(np.ascontiguousarray(a), F32).astype(BF16)
    return dict(w1=as_bf16(w1), w2=as_bf16(w2), wc=as_bf16(wc),
                cosf=as_f32(cosf), sinf=as_f32(sinf))


def _out_ffn_kernel(x_ref, ret_ref, four_ref, mod_ref, gpost_ref, gpre_ref, gfpost_ref,
                    wo_ref, wup_ref, taps_ref, wd_ref, o_ref, h_ref, u_ref, act_ref):
    n_groups = u_ref.shape[0]
    th = x_ref.shape[1] // n_groups
    d_ff = wd_ref.shape[0]
    gt_a, sh_f, sc_f, gt_f = mod_ref[0, 0:1], mod_ref[0, 1:2], mod_ref[0, 2:3], mod_ref[0, 3:4]
    groups = [slice(s * th, (s + 1) * th) for s in range(n_groups)]
    gain_a = gpost_ref[...] * gt_a
    gain_f = gpre_ref[...] * (1.0 + sc_f)
    gain_o = gfpost_ref[...] * gt_f

    for s, rows in enumerate(groups):
        mixed = [ret_ref[0, i, rows, :] for i in range(RET_HEADS)]
        mixed += [four_ref[0, i, rows, :] for i in range(FOURIER_GROUPS)]
        mix = jnp.dot(jnp.concatenate(mixed, axis=1), wo_ref[...], preferred_element_type=F32)
        x1 = x_ref[0, rows, :] + mix * _rms_scale(mix) * gain_a
        o_ref[0, rows, :] = x1
        h_ref[rows, :] = (x1 * _rms_scale(x1) * gain_f + sh_f).astype(BF16)
        u_ref[s, :, :, 0:U_PAD, :] = jnp.zeros(u_ref.shape[1:3] + (U_PAD, LANES), F32)
        u_ref[s, :, :, U_PAD + th:, :] = jnp.zeros(u_ref.shape[1:3] + (U_PAD, LANES), F32)

    lane_tiles = FF_CHUNK // LANES

    sub = lax.broadcasted_iota(jnp.int32, (8, LANES), 0)
    not_first = (sub != 0).astype(F32)
    not_last = (sub != 7).astype(F32)

    def conv(s, slot, lt, lo):
        w_prev, w_cur, w_next, bias = (taps_ref[i:i + 1, lo:lo + LANES] for i in range(4))
        blocks = []
        for r in range(0, th, GRID_W):
            prev = u_ref[s, slot, lt, U_PAD - 1 + r:U_PAD - 1 + r + GRID_W, :]
            nxt = u_ref[s, slot, lt, U_PAD + 1 + r:U_PAD + 1 + r + GRID_W, :]
            cur = u_ref[s, slot, lt, U_PAD + r:U_PAD + r + GRID_W, :]
            prev = jnp.concatenate([prev[:8] * not_first, prev[8:]], axis=0)
            nxt = jnp.concatenate([nxt[:GRID_W - 8], nxt[GRID_W - 8:] * not_last], axis=0)
            blocks.append(prev * w_prev + nxt * w_next + cur * w_cur + bias)
        return jnp.concatenate(blocks, axis=0)

    for j in range(d_ff // FF_CHUNK):
        cols = (j * FF_CHUNK, d_ff + j * FF_CHUNK)
        slots = (2 * j, 2 * j + 1)
        for s, rows in enumerate(groups):
            h = h_ref[rows, :]
            for slot, lo in zip(slots, cols):
                u = jnp.dot(h, wup_ref[:, lo:lo + FF_CHUNK], preferred_element_type=F32)
                for lt in range(lane_tiles):
                    u_ref[s, slot, lt, U_PAD:U_PAD + th, :] = u[:, lt * LANES:(lt + 1) * LANES]
            for lt in range(lane_tiles):
                a = conv(s, slots[0], lt, cols[0] + lt * LANES)
                b = conv(s, slots[1], lt, cols[1] + lt * LANES)
                lo = j * FF_CHUNK + lt * LANES
                act_ref[rows, lo:lo + LANES] = (_silu(a) * b).astype(BF16)

    for rows in groups:
        ffn = jnp.dot(act_ref[rows, :], wd_ref[...], preferred_element_type=F32)
        o_ref[0, rows, :] = o_ref[0, rows, :] + ffn * _rms_scale(ffn) * gain_o


def _out_ffn(x, ret, four, mod, gpost, gpre, gfpost, wo, wup, taps, wd, tm):
    bsz, seq, d = x.shape
    d_ff = wd.shape[0]
    tok = lambda w: pl.BlockSpec((1, tm, w), lambda b, i: (b, i, 0))
    split = lambda n: pl.BlockSpec((1, n, tm, LANES), lambda b, i: (b, 0, i, 0))
    vec = pl.BlockSpec((1, d), lambda b, i: (0, 0))
    const = lambda a: pl.BlockSpec(a.shape, lambda b, i: (0,) * a.ndim, pipeline_mode=pl.Buffered(1))
    return pl.pallas_call(
        _out_ffn_kernel,
        out_shape=jax.ShapeDtypeStruct((bsz, seq, d), F32),
        grid=(bsz, seq // tm),
        in_specs=[tok(d), split(RET_HEADS), split(FOURIER_GROUPS),
                  pl.BlockSpec((1, 4, d), lambda b, i: (b, 0, 0)), vec, vec, vec,
                  const(wo), const(wup), const(taps), const(wd)],
        out_specs=tok(d),
        scratch_shapes=[pltpu.VMEM((tm, d), BF16),
                        pltpu.VMEM((FF_GROUPS, 2 * (d_ff // FF_CHUNK), FF_CHUNK // LANES,
                                    tm // FF_GROUPS + 2 * U_PAD, LANES), F32),
                        pltpu.VMEM((tm, d_ff), BF16)],
        compiler_params=_params("arbitrary", "arbitrary"),
        name="out_ffn",
    )(x, ret, four, mod, gpost, gpre, gfpost, wo, wup, taps, wd)


def _layer(x, ctx, cond, consts, w_ada, b_ada, g_mix_pre, g_mix_post, g_ffn_pre, g_ffn_post,
           w_in, dec_f, dec_b, w_out, w_up, conv_w, conv_b, w_down):
    bsz, seq, d = x.shape
    d_ff = w_down.shape[0]
    row = lambda v: v.reshape(1, -1)

    mod = _adaln(cond, w_ada, row(b_ada))
    mod = mod.reshape(COND_ROWS, N_MOD, d)
    lat = mod[:bsz]
    cmod = mod[bsz]

    decf = jnp.broadcast_to(dec_f[:, None], (RET_HEADS, HEAD_DIM))
    decb = jnp.broadcast_to(dec_b[:, None], (RET_HEADS, HEAD_DIM))
    w_in_bf = w_in.astype(BF16)
    sf0, sb0 = _ctx_states(ctx, cmod[0:1], cmod[1:2], row(g_mix_pre),
                           w_in_bf[:, RET_W:3 * RET_W], decf, decb)

    q, k, v, gate, f = _in_proj(x, lat[:, 0:1], lat[:, 1:2], row(g_mix_pre), w_in_bf,
                                consts["cosf"], consts["sinf"], tm=1024)
    ret = _retention(q, k, v, gate, sf0, sb0, decf, decb)
    four = _fourier(f, consts)

    taps = jnp.concatenate([conv_w, conv_b[None], jnp.zeros((8 - CONV_WIDTH - 1, 2 * d_ff), F32)], axis=0)
    return _out_ffn(x, ret, four, lat[:, 2:6], row(g_mix_post), row(g_ffn_pre), row(g_ffn_post),
                    w_out.astype(BF16), w_up.astype(BF16), taps, w_down.astype(BF16), tm=512)


def kernel(x, c, ctx, c_ctx, w_ada, b_ada, g_mix_pre, g_mix_post, g_ffn_pre, g_ffn_post, w_in,
           ret_decay_fwd, ret_decay_bwd, w_out, w_up, conv_w, conv_b, w_down):
    bsz, seq, d = x.shape
    depth = w_ada.shape[0]
    assert depth == 1, "the context stream update of deeper stacks is not implemented"
    assert bsz + 1 <= COND_ROWS and seq % (GRID_W * GRID_W) == 0 and seq // GRID_W == GRID_W
    consts = _dft_constants(seq)
    cond = jnp.concatenate([c, c_ctx[None], jnp.zeros((COND_ROWS - bsz - 1, d), F32)], axis=0)
    layer = 0
    return _layer(x, ctx, cond, consts, w_ada[layer], b_ada[layer], g_mix_pre[layer], g_mix_post[layer],
                  g_ffn_pre[layer], g_ffn_post[layer], w_in[layer], ret_decay_fwd[layer],
                  ret_decay_bwd[layer], w_out[layer], w_up[layer], conv_w[layer], conv_b[layer],
                  w_down[layer])
```

```python
import math

import numpy as np
import jax
import jax.numpy as jnp
from jax import lax
from jax.experimental import pallas as pl
from jax.experimental.pallas import tpu as pltpu

F32 = jnp.float32
BF16 = jnp.bfloat16

RET_HEADS = 4
HEAD_DIM = 128
RET_W = RET_HEADS * HEAD_DIM
FOURIER_GROUPS = 4
GROUP_DIM = 128
FOURIER_W = FOURIER_GROUPS * GROUP_DIM
LANES = 128
GRID_W = 64
CHUNK = 256
ROPE_BASE = 10000.0
NORM_EPS = 1e-6
N_MOD = 6
CONV_WIDTH = 3
COND_ROWS = 24
FF_CHUNK = 256
RET_UNROLL = 16
IN_GROUPS = 2
FF_GROUPS = 2
U_PAD = 8
FFT_PITCH = GRID_W + 4
VMEM_LIMIT = 56 * 1024 * 1024


def _silu(x):
    return x / (1.0 + jnp.exp(-x))


def _rms_scale(x):
    return lax.rsqrt(jnp.mean(x * x, axis=-1, keepdims=True) + NORM_EPS)


def _rms_norm(x, g):
    return x * _rms_scale(x) * g


def _log_sigmoid(x):
    return jnp.minimum(x, 0.0) - jnp.log(1.0 + jnp.exp(-jnp.abs(x)))


def _params(*sem):
    return pltpu.CompilerParams(dimension_semantics=sem, vmem_limit_bytes=VMEM_LIMIT)


def _adaln_kernel(cond_ref, w_ref, b_ref, o_ref):
    s = _silu(cond_ref[...])
    o_ref[...] = jnp.dot(s, w_ref[...], preferred_element_type=F32) + b_ref[...]


def _adaln(cond, w, b):
    rows, d = cond.shape
    n = w.shape[1]
    tn = d
    return pl.pallas_call(
        _adaln_kernel,
        out_shape=jax.ShapeDtypeStruct((rows, n), F32),
        grid=(n // tn,),
        in_specs=[pl.BlockSpec((rows, d), lambda j: (0, 0)),
                  pl.BlockSpec((d, tn), lambda j: (0, j)),
                  pl.BlockSpec((1, tn), lambda j: (0, j))],
        out_specs=pl.BlockSpec((rows, tn), lambda j: (0, j)),
        compiler_params=_params("arbitrary"),
        name="adaln",
    )(cond, w, b)


def _ctx_kernel(ctx_ref, sh_ref, sc_ref, g_ref, w_ref, decf_ref, decb_ref, sf_ref, sb_ref):
    n_ctx = ctx_ref.shape[1]
    h = _rms_norm(ctx_ref[0], g_ref[...]) * (1.0 + sc_ref[...]) + sh_ref[...]
    kv = jnp.dot(h.astype(BF16), w_ref[...], preferred_element_type=F32)
    lgf = _log_sigmoid(decf_ref[...])
    lgb = _log_sigmoid(decb_ref[...])
    m = lax.broadcasted_iota(jnp.int32, (n_ctx, HEAD_DIM), 0).astype(F32)
    for hd in range(RET_HEADS):
        k = kv[:, hd * HEAD_DIM:(hd + 1) * HEAD_DIM] * (HEAD_DIM ** -0.5)
        v = kv[:, RET_W + hd * HEAD_DIM:RET_W + (hd + 1) * HEAD_DIM].astype(BF16)
        w_f = jnp.exp((n_ctx - 1.0 - m) * lgf[hd:hd + 1, :])
        w_b = jnp.exp(m * lgb[hd:hd + 1, :])
        dn = (((0,), (0,)), ((), ()))
        sf_ref[0, hd] = lax.dot_general((k * w_f).astype(BF16), v, dn, preferred_element_type=F32)
        sb_ref[0, hd] = lax.dot_general((k * w_b).astype(BF16), v, dn, preferred_element_type=F32)


def _ctx_states(ctx, sh, sc, g, w_kv, decf, decb):
    bsz, n_ctx, d = ctx.shape
    st = jax.ShapeDtypeStruct((bsz, RET_HEADS, HEAD_DIM, HEAD_DIM), F32)
    st_spec = pl.BlockSpec((1, RET_HEADS, HEAD_DIM, HEAD_DIM), lambda b: (b, 0, 0, 0))
    vec = pl.BlockSpec((1, d), lambda b: (0, 0))
    dec = pl.BlockSpec((RET_HEADS, HEAD_DIM), lambda b: (0, 0))
    return pl.pallas_call(
        _ctx_kernel,
        out_shape=(st, st),
        grid=(bsz,),
        in_specs=[pl.BlockSpec((1, n_ctx, d), lambda b: (b, 0, 0)), vec, vec, vec,
                  pl.BlockSpec(w_kv.shape, lambda b: (0, 0)), dec, dec],
        out_specs=(st_spec, st_spec),
        compiler_params=_params("arbitrary"),
        name="ctx_state",
    )(ctx, sh, sc, g, w_kv, decf, decb)


def _in_proj_kernel(x_ref, sh_ref, sc_ref, g_ref, w_ref, cos_ref, sin_ref,
                    q_ref, k_ref, v_ref, gate_ref, f_ref):
    th = x_ref.shape[1] // IN_GROUPS
    gain = g_ref[...] * (1.0 + sc_ref[0])
    for s in range(IN_GROUPS):
        rows = slice(s * th, (s + 1) * th)
        x = x_ref[0, rows, :]
        h = x * _rms_scale(x) * gain + sh_ref[0]
        proj = jnp.dot(h.astype(BF16), w_ref[...], preferred_element_type=F32)
        cosf = cos_ref[rows, :]
        sinf = sin_ref[rows, :]
        for hd in range(RET_HEADS):
            lo = hd * HEAD_DIM
            q = proj[:, lo:lo + HEAD_DIM]
            k = proj[:, RET_W + lo:RET_W + lo + HEAD_DIM] * (HEAD_DIM ** -0.5)
            q_ref[0, hd, rows, :] = (q * cosf + pltpu.roll(q, HEAD_DIM // 2, 1) * sinf).astype(BF16)
            k_ref[0, hd, rows, :] = (k * cosf + pltpu.roll(k, HEAD_DIM // 2, 1) * sinf).astype(BF16)
            v_ref[0, hd, rows, :] = proj[:, 2 * RET_W + lo:2 * RET_W + lo + HEAD_DIM].astype(BF16)
            gate_ref[0, hd, rows, :] = proj[:, 3 * RET_W + lo:3 * RET_W + lo + HEAD_DIM].astype(BF16)
        for g in range(FOURIER_GROUPS):
            lo = 4 * RET_W + g * GROUP_DIM
            f_ref[0, g, rows, :] = proj[:, lo:lo + GROUP_DIM].astype(BF16)


def _in_proj(x, sh, sc, g, w_in, cosf, sinf, tm):
    bsz, seq, d = x.shape
    mod = pl.BlockSpec((1, 1, d), lambda b, i: (b, 0, 0))
    split = jax.ShapeDtypeStruct((bsz, RET_HEADS, seq, HEAD_DIM), BF16)
    split_spec = pl.BlockSpec((1, RET_HEADS, tm, HEAD_DIM), lambda b, i: (b, 0, i, 0))
    rope = pl.BlockSpec((tm, HEAD_DIM), lambda b, i: (i, 0))
    return pl.pallas_call(
        _in_proj_kernel,
        out_shape=(split,) * 5,
        grid=(bsz, seq // tm),
        in_specs=[pl.BlockSpec((1, tm, d), lambda b, i: (b, i, 0)), mod, mod,
                  pl.BlockSpec((1, d), lambda b, i: (0, 0)),
                  pl.BlockSpec(w_in.shape, lambda b, i: (0, 0)), rope, rope],
        out_specs=(split_spec,) * 5,
        compiler_params=_params("arbitrary", "arbitrary"),
        name="in_proj",
    )(x, sh, sc, g, w_in, cosf, sinf)


def _retention_kernel(q_ref, k_ref, v_ref, gate_ref, sf0_ref, sb0_ref, decf_ref, decb_ref,
                      o_ref, kvf_scr, kvb_scr, sf_scr, sb_scr):
    hd = pl.program_id(1)
    n_chunks = q_ref.shape[2] // CHUNK
    lgf = _log_sigmoid(decf_ref[pl.ds(hd, 1), :])
    lgb = _log_sigmoid(decb_ref[pl.ds(hd, 1), :])
    pos = lax.broadcasted_iota(jnp.int32, (CHUNK, HEAD_DIM), 0).astype(F32)
    diff = (lax.broadcasted_iota(jnp.int32, (CHUNK, CHUNK), 0)
            - lax.broadcasted_iota(jnp.int32, (CHUNK, CHUNK), 1)).astype(F32)
    lgf_c = jnp.concatenate([lgf] * (CHUNK // HEAD_DIM), axis=1)
    lgb_c = jnp.concatenate([lgb] * (CHUNK // HEAD_DIM), axis=1)
    zeta_f = jnp.exp((CHUNK - 1.0 - pos) * lgf)
    zeta_b = jnp.exp(pos * lgb)
    xi_f = jnp.exp((pos + 1.0) * lgf)
    xi_b = jnp.exp((CHUNK - pos) * lgb)
    decay = jnp.exp(jnp.where(diff >= 0, diff * lgf_c, -diff * lgb_c))
    cdec_f = jnp.exp(CHUNK * lgf)
    cdec_b = jnp.exp(CHUNK * lgb)
    tn = (((0,), (0,)), ((), ()))
    nt = (((1,), (1,)), ((), ()))

    def chunk_kv(n, carry):
        rows = pl.ds(pl.multiple_of(n * CHUNK, CHUNK), CHUNK)
        k = k_ref[0, 0, rows, :].astype(F32)
        v = v_ref[0, 0, rows, :]
        kvf_scr[n] = lax.dot_general((k * zeta_f).astype(BF16), v, tn, preferred_element_type=F32)
        kvb_scr[n] = lax.dot_general((k * zeta_b).astype(BF16), v, tn, preferred_element_type=F32)
        return carry

    lax.fori_loop(0, n_chunks, chunk_kv, 0, unroll=RET_UNROLL)

    def scan_f(n, state):
        sf_scr[n] = state.astype(BF16)
        return state * cdec_f + kvf_scr[n]

    lax.fori_loop(0, n_chunks, scan_f, sf0_ref[0, 0])

    def scan_b(i, state):
        n = n_chunks - 1 - i
        sb_scr[n] = state.astype(BF16)
        return state * cdec_b + kvb_scr[n]

    lax.fori_loop(0, n_chunks, scan_b, sb0_ref[0, 0])

    def chunk_out(n, carry):
        rows = pl.ds(pl.multiple_of(n * CHUNK, CHUNK), CHUNK)
        q = q_ref[0, 0, rows, :]
        k = k_ref[0, 0, rows, :]
        v = v_ref[0, 0, rows, :]
        scores = lax.dot_general(q, k, nt, preferred_element_type=F32) * decay
        qf = q.astype(F32)
        o = jnp.dot(scores.astype(BF16), v, preferred_element_type=F32)
        o = o + jnp.dot((qf * xi_f).astype(BF16), sf_scr[n], preferred_element_type=F32)
        o = o + jnp.dot((qf * xi_b).astype(BF16), sb_scr[n], preferred_element_type=F32)
        o = o * lax.rsqrt(jnp.mean(o * o, axis=-1, keepdims=True) + NORM_EPS)
        o_ref[0, 0, rows, :] = (o * _silu(gate_ref[0, 0, rows, :].astype(F32))).astype(BF16)
        return carry

    lax.fori_loop(0, n_chunks, chunk_out, 0, unroll=RET_UNROLL)


def _retention(q, k, v, gate, sf0, sb0, decf, decb):
    bsz, _, seq, _ = q.shape
    n_chunks = seq // CHUNK
    tok = pl.BlockSpec((1, 1, seq, HEAD_DIM), lambda b, h: (b, h, 0, 0))
    st = pl.BlockSpec((1, 1, HEAD_DIM, HEAD_DIM), lambda b, h: (b, h, 0, 0))
    dec = pl.BlockSpec((RET_HEADS, HEAD_DIM), lambda b, h: (0, 0))
    return pl.pallas_call(
        _retention_kernel,
        out_shape=jax.ShapeDtypeStruct(q.shape, BF16),
        grid=(bsz, RET_HEADS),
        in_specs=[tok, tok, tok, tok, st, st, dec, dec],
        out_specs=tok,
        scratch_shapes=[pltpu.VMEM((n_chunks, HEAD_DIM, HEAD_DIM), F32),
                        pltpu.VMEM((n_chunks, HEAD_DIM, HEAD_DIM), F32),
                        pltpu.VMEM((n_chunks, HEAD_DIM, HEAD_DIM), BF16),
                        pltpu.VMEM((n_chunks, HEAD_DIM, HEAD_DIM), BF16)],
        compiler_params=_params("arbitrary", "arbitrary"),
        name="retention",
    )(q, k, v, gate, sf0, sb0, decf, decb)


def _fourier_kernel(f_ref, w1_ref, w2_ref, wc_ref, o_ref,
                    xs, ar_s, ai_s, xr_s, xi_s):
    gh = xs.shape[0] // FFT_PITCH
    for t1 in range(gh):
        xs[pl.ds(t1 * FFT_PITCH, GRID_W), :] = f_ref[0, 0, pl.ds(t1 * GRID_W, GRID_W), :].astype(F32)
    for t2 in range(0, GRID_W, 2):
        x = jnp.concatenate([xs[pl.ds(t2 + u, gh, stride=FFT_PITCH), :] for u in range(2)], axis=1)
        r = jnp.dot(w1_ref[...], x.astype(BF16), preferred_element_type=F32)
        for u in range(2):
            lanes = slice(u * GROUP_DIM, (u + 1) * GROUP_DIM)
            ar_s[pl.ds(t2 + u, gh, stride=FFT_PITCH), :] = r[:gh, lanes]
            ai_s[pl.ds(t2 + u, gh, stride=FFT_PITCH), :] = r[gh:, lanes]
    for k1 in range(gh):
        rows = pl.ds(k1 * FFT_PITCH, GRID_W)
        stacked = jnp.concatenate([ar_s[rows, :], ai_s[rows, :]], axis=0).astype(BF16)
        r = jnp.dot(w2_ref[k1], stacked, preferred_element_type=F32)
        xr_s[pl.ds(k1, GRID_W, stride=FFT_PITCH), :] = r[:GRID_W]
        xi_s[pl.ds(k1, GRID_W, stride=FFT_PITCH), :] = r[GRID_W:]
    for kb in range(0, GRID_W, 8):
        parts = []
        for k2 in range(kb, kb + 8):
            rows = pl.ds(k2 * FFT_PITCH, gh)
            parts.append(jnp.concatenate([xr_s[rows, :], xi_s[rows, :]], axis=1).astype(BF16))
        res = jnp.dot(jnp.concatenate(parts, axis=0), wc_ref[...], preferred_element_type=F32)
        o_ref[0, 0, pl.ds(kb * gh, 8 * gh), :] = res.astype(BF16)


def _fourier(f, consts):
    bsz, n_groups, seq, _ = f.shape
    gh = seq // GRID_W
    tok = pl.BlockSpec((1, 1, seq, GROUP_DIM), lambda b, g: (b, g, 0, 0))
    const = lambda a: pl.BlockSpec(a.shape, lambda b, g: (0,) * a.ndim)
    names = ("w1", "w2", "wc")
    pitched = pltpu.VMEM((gh * FFT_PITCH, GROUP_DIM), F32)
    return pl.pallas_call(
        _fourier_kernel,
        out_shape=jax.ShapeDtypeStruct(f.shape, BF16),
        grid=(bsz, n_groups),
        in_specs=[tok] + [const(consts[n]) for n in names],
        out_specs=tok,
        scratch_shapes=[pitched] * 5,
        compiler_params=_params("arbitrary", "arbitrary"),
        name="fourier",
    )(f, *[consts[n] for n in names])


def _dft_constants(seq):
    gh = seq // GRID_W
    def cs(n_rows, n_cols, period):
        ang = 2.0 * np.pi * np.outer(np.arange(n_rows), np.arange(n_cols)) / period
        return np.cos(ang), np.sin(ang)
    c1, s1 = cs(gh, gh, gh)
    w1 = np.concatenate([c1, -s1], axis=0) / gh
    c2, s2 = cs(GRID_W, GRID_W, GRID_W)
    cc, sc = cs(GROUP_DIM, GROUP_DIM, GROUP_DIM)
    wc = np.concatenate([cc, sc], axis=0) / math.sqrt(GROUP_DIM)
    twc, tws = cs(gh, GRID_W, seq)
    p = c2[None] * twc[:, None, :] - s2[None] * tws[:, None, :]
    q = s2[None] * twc[:, None, :] + c2[None] * tws[:, None, :]
    w2 = np.concatenate([np.concatenate([p, q], axis=2), np.concatenate([-q, p], axis=2)], axis=1)
    w2 = w2 / math.sqrt(GRID_W * gh) * gh
    t = np.arange(seq)
    n_freq = HEAD_DIM // 4
    freqs = ROPE_BASE ** (-np.arange(n_freq, dtype=np.float32) / n_freq)
    rang = np.concatenate([(t // GRID_W).astype(np.float32)[:, None] * freqs,
                           (t % GRID_W).astype(np.float32)[:, None] * freqs], axis=-1).astype(np.float32)
    cosf = np.concatenate([np.cos(rang), np.cos(rang)], axis=-1)
    sinf = np.concatenate([-np.sin(rang), np.sin(rang)], axis=-1)
    as_f32 = lambda a: jnp.asarray(np.ascontiguousarray(a), F32)
    as_bf16 = lambda a: jnp.asarray(np.ascontiguousarray(a), F32).astype(BF16)
    return dict(w1=as_bf16(w1), w2=as_bf16(w2), wc=as_bf16(wc),
                cosf=as_f32(cosf), sinf=as_f32(sinf))


def _out_ffn_kernel(x_ref, ret_ref, four_ref, mod_ref, gpost_ref, gpre_ref, gfpost_ref,
                    wo_ref, wup_ref, taps_ref, wd_ref, o_ref, h_ref, u_ref, act_ref):
    tm = x_ref.shape[1]
    th = tm // FF_GROUPS
    d_ff = wd_ref.shape[0]
    gt_a, sh_f, sc_f, gt_f = mod_ref[0, 0:1], mod_ref[0, 1:2], mod_ref[0, 2:3], mod_ref[0, 3:4]
    groups = [slice(s * th, (s + 1) * th) for s in range(FF_GROUPS)]
    gain_a = gpost_ref[...] * gt_a
    gain_f = gpre_ref[...] * (1.0 + sc_f)
    gain_o = gfpost_ref[...] * gt_f

    for rows in groups:
        mixed = [ret_ref[0, i, rows, :] for i in range(RET_HEADS)]
        mixed += [four_ref[0, i, rows, :] for i in range(FOURIER_GROUPS)]
        mix = jnp.dot(jnp.concatenate(mixed, axis=1), wo_ref[...], preferred_element_type=F32)
        x1 = x_ref[0, rows, :] + mix * _rms_scale(mix) * gain_a
        o_ref[0, rows, :] = x1
        h_ref[rows, :] = (x1 * _rms_scale(x1) * gain_f + sh_f).astype(BF16)
    u_ref[:, :, 0:U_PAD, :] = jnp.zeros(u_ref.shape[:2] + (U_PAD, LANES), F32)
    u_ref[:, :, U_PAD + tm:, :] = jnp.zeros(u_ref.shape[:2] + (U_PAD, LANES), F32)

    lane_tiles = FF_CHUNK // LANES
    sub = lax.broadcasted_iota(jnp.int32, (8, LANES), 0)
    not_first = (sub != 0).astype(F32)
    not_last = (sub != 7).astype(F32)

    def conv(slot, lt, lo):
        w_prev, w_cur, w_next, bias = (taps_ref[i:i + 1, lo:lo + LANES] for i in range(4))
        blocks = []
        for r in range(0, tm, GRID_W):
            prev = u_ref[slot, lt, U_PAD - 1 + r:U_PAD - 1 + r + GRID_W, :]
            nxt = u_ref[slot, lt, U_PAD + 1 + r:U_PAD + 1 + r + GRID_W, :]
            cur = u_ref[slot, lt, U_PAD + r:U_PAD + r + GRID_W, :]
            prev = jnp.concatenate([prev[:8] * not_first, prev[8:]], axis=0)
            nxt = jnp.concatenate([nxt[:GRID_W - 8], nxt[GRID_W - 8:] * not_last], axis=0)
            blocks.append(prev * w_prev + nxt * w_next + cur * w_cur + bias)
        return jnp.concatenate(blocks, axis=0)

    for j in range(d_ff // FF_CHUNK):
        cols = (j * FF_CHUNK, d_ff + j * FF_CHUNK)
        slots = (2 * j, 2 * j + 1)
        h = h_ref[...]
        for slot, lo in zip(slots, cols):
            u = jnp.dot(h, wup_ref[:, lo:lo + FF_CHUNK], preferred_element_type=F32)
            for lt in range(lane_tiles):
                u_ref[slot, lt, U_PAD:U_PAD + tm, :] = u[:, lt * LANES:(lt + 1) * LANES]
        for lt in range(lane_tiles):
            a = conv(slots[0], lt, cols[0] + lt * LANES)
            b = conv(slots[1], lt, cols[1] + lt * LANES)
            lo = j * FF_CHUNK + lt * LANES
            act_ref[:, lo:lo + LANES] = (_silu(a) * b).astype(BF16)

    for rows in groups:
        ffn = jnp.dot(act_ref[rows, :], wd_ref[...], preferred_element_type=F32)
        o_ref[0, rows, :] = o_ref[0, rows, :] + ffn * _rms_scale(ffn) * gain_o


def _out_ffn(x, ret, four, mod, gpost, gpre, gfpost, wo, wup, taps, wd, tm):
    bsz, seq, d = x.shape
    d_ff = wd.shape[0]
    tok = lambda w: pl.BlockSpec((1, tm, w), lambda b, i: (b, i, 0))
    split = lambda n: pl.BlockSpec((1, n, tm, LANES), lambda b, i: (b, 0, i, 0))
    vec = pl.BlockSpec((1, d), lambda b, i: (0, 0))
    const = lambda a: pl.BlockSpec(a.shape, lambda b, i: (0,) * a.ndim, pipeline_mode=pl.Buffered(1))
    return pl.pallas_call(
        _out_ffn_kernel,
        out_shape=jax.ShapeDtypeStruct((bsz, seq, d), F32),
        grid=(bsz, seq // tm),
        in_specs=[tok(d), split(RET_HEADS), split(FOURIER_GROUPS),
                  pl.BlockSpec((1, 4, d), lambda b, i: (b, 0, 0)), vec, vec, vec,
                  const(wo), const(wup), const(taps), const(wd)],
        out_specs=tok(d),
        scratch_shapes=[pltpu.VMEM((tm, d), BF16),
                        pltpu.VMEM((2 * (d_ff // FF_CHUNK), FF_CHUNK // LANES, tm + 2 * U_PAD, LANES), F32),
                        pltpu.VMEM((tm, d_ff), BF16)],
        compiler_params=_params("arbitrary", "arbitrary"),
        name="out_ffn",
    )(x, ret, four, mod, gpost, gpre, gfpost, wo, wup, taps, wd)


def _layer(x, ctx, cond, consts, w_ada, b_ada, g_mix_pre, g_mix_post, g_ffn_pre, g_ffn_post,
           w_in, dec_f, dec_b, w_out, w_up, conv_w, conv_b, w_down):
    bsz, seq, d = x.shape
    d_ff = w_down.shape[0]
    row = lambda v: v.reshape(1, -1)

    mod = _adaln(cond, w_ada, row(b_ada))
    mod = mod.reshape(COND_ROWS, N_MOD, d)
    lat = mod[:bsz]
    cmod = mod[bsz]

    decf = jnp.broadcast_to(dec_f[:, None], (RET_HEADS, HEAD_DIM))
    decb = jnp.broadcast_to(dec_b[:, None], (RET_HEADS, HEAD_DIM))
    w_in_bf = w_in.astype(BF16)
    sf0, sb0 = _ctx_states(ctx, cmod[0:1], cmod[1:2], row(g_mix_pre),
                           w_in_bf[:, RET_W:3 * RET_W], decf, decb)

    q, k, v, gate, f = _in_proj(x, lat[:, 0:1], lat[:, 1:2], row(g_mix_pre), w_in_bf,
                                consts["cosf"], consts["sinf"], tm=1024)
    ret = _retention(q, k, v, gate, sf0, sb0, decf, decb)
    four = _fourier(f, consts)

    taps = jnp.concatenate([conv_w, conv_b[None], jnp.zeros((8 - CONV_WIDTH - 1, 2 * d_ff), F32)], axis=0)
    return _out_ffn(x, ret, four, lat[:, 2:6], row(g_mix_post), row(g_ffn_pre), row(g_ffn_post),
                    w_out.astype(BF16), w_up.astype(BF16), taps, w_down.astype(BF16), tm=512)


def kernel(x, c, ctx, c_ctx, w_ada, b_ada, g_mix_pre, g_mix_post, g_ffn_pre, g_ffn_post, w_in,
           ret_decay_fwd, ret_decay_bwd, w_out, w_up, conv_w, conv_b, w_down):
    bsz, seq, d = x.shape
    depth = w_ada.shape[0]
    assert depth == 1, "the context stream update of deeper stacks is not implemented"
    assert bsz + 1 <= COND_ROWS and seq % (GRID_W * GRID_W) == 0 and seq // GRID_W == GRID_W
    consts = _dft_constants(seq)
    cond = jnp.concatenate([c, c_ctx[None], jnp.zeros((COND_ROWS - bsz - 1, d), F32)], axis=0)
    layer = 0
    return _layer(x, ctx, cond, consts, w_ada[layer], b_ada[layer], g_mix_pre[layer], g_mix_post[layer],
                  g_ffn_pre[layer], g_ffn_post[layer], w_in[layer], ret_decay_fwd[layer],
                  ret_decay_bwd[layer], w_out[layer], w_up[layer], conv_w[layer], conv_b[layer],
                  w_down[layer])
```

```python
import math

import numpy as np
import jax
import jax.numpy as jnp
from jax import lax
from jax.experimental import pallas as pl
from jax.experimental.pallas import tpu as pltpu

F32 = jnp.float32
BF16 = jnp.bfloat16

RET_HEADS = 4
HEAD_DIM = 128
RET_W = RET_HEADS * HEAD_DIM
FOURIER_GROUPS = 4
GROUP_DIM = 128
LANES = 128
SUBLANES = 8
GRID_W = 64
CHUNK = 256
ROPE_BASE = 10000.0
NORM_EPS = 1e-6
N_MOD = 6
CONV_WIDTH = 3
COND_ROWS = 24
FF_CHUNK = 256
RET_UNROLL = 16
IN_GROUPS = 2
FF_GROUPS = 2
U_PAD = SUBLANES
FFT_PITCH = GRID_W + 4
FFT_K2_BATCH = 8
VMEM_LIMIT = 56 * 1024 * 1024


def _silu(x):
    return x / (1.0 + jnp.exp(-x))


def _rms_scale(x):
    return lax.rsqrt(jnp.mean(x * x, axis=-1, keepdims=True) + NORM_EPS)


def _rms_norm(x, g):
    return x * _rms_scale(x) * g


def _log_sigmoid(x):
    return jnp.minimum(x, 0.0) - jnp.log(1.0 + jnp.exp(-jnp.abs(x)))


def _params(*sem):
    return pltpu.CompilerParams(dimension_semantics=sem, vmem_limit_bytes=VMEM_LIMIT)


def _adaln_kernel(cond_ref, w_ref, b_ref, o_ref):
    s = _silu(cond_ref[...])
    o_ref[...] = jnp.dot(s, w_ref[...], preferred_element_type=F32) + b_ref[...]


def _adaln(cond, w, b):
    rows, d = cond.shape
    n = w.shape[1]
    tn = d
    return pl.pallas_call(
        _adaln_kernel,
        out_shape=jax.ShapeDtypeStruct((rows, n), F32),
        grid=(n // tn,),
        in_specs=[pl.BlockSpec((rows, d), lambda j: (0, 0)),
                  pl.BlockSpec((d, tn), lambda j: (0, j)),
                  pl.BlockSpec((1, tn), lambda j: (0, j))],
        out_specs=pl.BlockSpec((rows, tn), lambda j: (0, j)),
        compiler_params=_params("arbitrary"),
        name="adaln",
    )(cond, w, b)


def _ctx_kernel(ctx_ref, sh_ref, sc_ref, g_ref, w_ref, decf_ref, decb_ref, sf_ref, sb_ref):
    n_ctx = ctx_ref.shape[1]
    h = _rms_norm(ctx_ref[0], g_ref[...]) * (1.0 + sc_ref[...]) + sh_ref[...]
    kv = jnp.dot(h.astype(BF16), w_ref[...], preferred_element_type=F32)
    lgf = _log_sigmoid(decf_ref[...])
    lgb = _log_sigmoid(decb_ref[...])
    m = lax.broadcasted_iota(jnp.int32, (n_ctx, HEAD_DIM), 0).astype(F32)
    for hd in range(RET_HEADS):
        k = kv[:, hd * HEAD_DIM:(hd + 1) * HEAD_DIM] * (HEAD_DIM ** -0.5)
        v = kv[:, RET_W + hd * HEAD_DIM:RET_W + (hd + 1) * HEAD_DIM].astype(BF16)
        w_f = jnp.exp((n_ctx - 1.0 - m) * lgf[hd:hd + 1, :])
        w_b = jnp.exp(m * lgb[hd:hd + 1, :])
        dn = (((0,), (0,)), ((), ()))
        sf_ref[0, hd] = lax.dot_general((k * w_f).astype(BF16), v, dn, preferred_element_type=F32)
        sb_ref[0, hd] = lax.dot_general((k * w_b).astype(BF16), v, dn, preferred_element_type=F32)


def _ctx_states(ctx, sh, sc, g, w_kv, decf, decb):
    bsz, n_ctx, d = ctx.shape
    st = jax.ShapeDtypeStruct((bsz, RET_HEADS, HEAD_DIM, HEAD_DIM), F32)
    st_spec = pl.BlockSpec((1, RET_HEADS, HEAD_DIM, HEAD_DIM), lambda b: (b, 0, 0, 0))
    vec = pl.BlockSpec((1, d), lambda b: (0, 0))
    dec = pl.BlockSpec((RET_HEADS, HEAD_DIM), lambda b: (0, 0))
    return pl.pallas_call(
        _ctx_kernel,
        out_shape=(st, st),
        grid=(bsz,),
        in_specs=[pl.BlockSpec((1, n_ctx, d), lambda b: (b, 0, 0)), vec, vec, vec,
                  pl.BlockSpec(w_kv.shape, lambda b: (0, 0)), dec, dec],
        out_specs=(st_spec, st_spec),
        compiler_params=_params("arbitrary"),
        name="ctx_state",
    )(ctx, sh, sc, g, w_kv, decf, decb)


def _in_proj_kernel(x_ref, sh_ref, sc_ref, g_ref, w_ref, cos_ref, sin_ref,
                    q_ref, k_ref, v_ref, gate_ref, f_ref):
    th = x_ref.shape[1] // IN_GROUPS
    gain = g_ref[...] * (1.0 + sc_ref[0])
    for s in range(IN_GROUPS):
        rows = slice(s * th, (s + 1) * th)
        x = x_ref[0, rows, :]
        h = x * _rms_scale(x) * gain + sh_ref[0]
        proj = jnp.dot(h.astype(BF16), w_ref[...], preferred_element_type=F32)
        cosf = cos_ref[rows, :]
        sinf = sin_ref[rows, :]
        for hd in range(RET_HEADS):
            lo = hd * HEAD_DIM
            q = proj[:, lo:lo + HEAD_DIM]
            k = proj[:, RET_W + lo:RET_W + lo + HEAD_DIM] * (HEAD_DIM ** -0.5)
            q_ref[0, hd, rows, :] = (q * cosf + pltpu.roll(q, HEAD_DIM // 2, 1) * sinf).astype(BF16)
            k_ref[0, hd, rows, :] = (k * cosf + pltpu.roll(k, HEAD_DIM // 2, 1) * sinf).astype(BF16)
            v_ref[0, hd, rows, :] = proj[:, 2 * RET_W + lo:2 * RET_W + lo + HEAD_DIM].astype(BF16)
            gate_ref[0, hd, rows, :] = proj[:, 3 * RET_W + lo:3 * RET_W + lo + HEAD_DIM].astype(BF16)
        for g in range(FOURIER_GROUPS):
            lo = 4 * RET_W + g * GROUP_DIM
            f_ref[0, g, rows, :] = proj[:, lo:lo + GROUP_DIM].astype(BF16)


def _in_proj(x, sh, sc, g, w_in, cosf, sinf, tm):
    bsz, seq, d = x.shape
    mod = pl.BlockSpec((1, 1, d), lambda b, i: (b, 0, 0))
    split = jax.ShapeDtypeStruct((bsz, RET_HEADS, seq, HEAD_DIM), BF16)
    split_spec = pl.BlockSpec((1, RET_HEADS, tm, HEAD_DIM), lambda b, i: (b, 0, i, 0))
    rope = pl.BlockSpec((tm, HEAD_DIM), lambda b, i: (i, 0))
    return pl.pallas_call(
        _in_proj_kernel,
        out_shape=(split,) * 5,
        grid=(bsz, seq // tm),
        in_specs=[pl.BlockSpec((1, tm, d), lambda b, i: (b, i, 0)), mod, mod,
                  pl.BlockSpec((1, d), lambda b, i: (0, 0)),
                  pl.BlockSpec(w_in.shape, lambda b, i: (0, 0)), rope, rope],
        out_specs=(split_spec,) * 5,
        compiler_params=_params("arbitrary", "arbitrary"),
        name="in_proj",
    )(x, sh, sc, g, w_in, cosf, sinf)


def _retention_kernel(q_ref, k_ref, v_ref, gate_ref, sf0_ref, sb0_ref, decf_ref, decb_ref,
                      o_ref, kvf_scr, kvb_scr, sf_scr, sb_scr):
    hd = pl.program_id(1)
    n_chunks = q_ref.shape[2] // CHUNK
    lgf = _log_sigmoid(decf_ref[pl.ds(hd, 1), :])
    lgb = _log_sigmoid(decb_ref[pl.ds(hd, 1), :])
    pos = lax.broadcasted_iota(jnp.int32, (CHUNK, HEAD_DIM), 0).astype(F32)
    diff = (lax.broadcasted_iota(jnp.int32, (CHUNK, CHUNK), 0)
            - lax.broadcasted_iota(jnp.int32, (CHUNK, CHUNK), 1)).astype(F32)
    lgf_c = jnp.concatenate([lgf] * (CHUNK // HEAD_DIM), axis=1)
    lgb_c = jnp.concatenate([lgb] * (CHUNK // HEAD_DIM), axis=1)
    zeta_f = jnp.exp((CHUNK - 1.0 - pos) * lgf)
    zeta_b = jnp.exp(pos * lgb)
    xi_f = jnp.exp((pos + 1.0) * lgf)
    xi_b = jnp.exp((CHUNK - pos) * lgb)
    decay = jnp.exp(jnp.where(diff >= 0, diff * lgf_c, -diff * lgb_c))
    cdec_f = jnp.exp(CHUNK * lgf)
    cdec_b = jnp.exp(CHUNK * lgb)
    tn = (((0,), (0,)), ((), ()))
    nt = (((1,), (1,)), ((), ()))

    def chunk_kv(n, carry):
        rows = pl.ds(pl.multiple_of(n * CHUNK, CHUNK), CHUNK)
        k = k_ref[0, 0, rows, :].astype(F32)
        v = v_ref[0, 0, rows, :]
        kvf_scr[n] = lax.dot_general((k * zeta_f).astype(BF16), v, tn, preferred_element_type=F32)
        kvb_scr[n] = lax.dot_general((k * zeta_b).astype(BF16), v, tn, preferred_element_type=F32)
        return carry

    lax.fori_loop(0, n_chunks, chunk_kv, 0, unroll=RET_UNROLL)

    def scan_f(n, state):
        sf_scr[n] = state.astype(BF16)
        return state * cdec_f + kvf_scr[n]

    lax.fori_loop(0, n_chunks, scan_f, sf0_ref[0, 0])

    def scan_b(i, state):
        n = n_chunks - 1 - i
        sb_scr[n] = state.astype(BF16)
        return state * cdec_b + kvb_scr[n]

    lax.fori_loop(0, n_chunks, scan_b, sb0_ref[0, 0])

    def chunk_out(n, carry):
        rows = pl.ds(pl.multiple_of(n * CHUNK, CHUNK), CHUNK)
        q = q_ref[0, 0, rows, :]
        k = k_ref[0, 0, rows, :]
        v = v_ref[0, 0, rows, :]
        scores = lax.dot_general(q, k, nt, preferred_element_type=F32) * decay
        qf = q.astype(F32)
        o = jnp.dot(scores.astype(BF16), v, preferred_element_type=F32)
        o = o + jnp.dot((qf * xi_f).astype(BF16), sf_scr[n], preferred_element_type=F32)
        o = o + jnp.dot((qf * xi_b).astype(BF16), sb_scr[n], preferred_element_type=F32)
        o = o * lax.rsqrt(jnp.mean(o * o, axis=-1, keepdims=True) + NORM_EPS)
        o_ref[0, 0, rows, :] = (o * _silu(gate_ref[0, 0, rows, :].astype(F32))).astype(BF16)
        return carry

    lax.fori_loop(0, n_chunks, chunk_out, 0, unroll=RET_UNROLL)


def _retention(q, k, v, gate, sf0, sb0, decf, decb):
    bsz, _, seq, _ = q.shape
    n_chunks = seq // CHUNK
    tok = pl.BlockSpec((1, 1, seq, HEAD_DIM), lambda b, h: (b, h, 0, 0))
    st = pl.BlockSpec((1, 1, HEAD_DIM, HEAD_DIM), lambda b, h: (b, h, 0, 0))
    dec = pl.BlockSpec((RET_HEADS, HEAD_DIM), lambda b, h: (0, 0))
    return pl.pallas_call(
        _retention_kernel,
        out_shape=jax.ShapeDtypeStruct(q.shape, BF16),
        grid=(bsz, RET_HEADS),
        in_specs=[tok, tok, tok, tok, st, st, dec, dec],
        out_specs=tok,
        scratch_shapes=[pltpu.VMEM((n_chunks, HEAD_DIM, HEAD_DIM), F32),
                        pltpu.VMEM((n_chunks, HEAD_DIM, HEAD_DIM), F32),
                        pltpu.VMEM((n_chunks, HEAD_DIM, HEAD_DIM), BF16),
                        pltpu.VMEM((n_chunks, HEAD_DIM, HEAD_DIM), BF16)],
        compiler_params=_params("arbitrary", "arbitrary"),
        name="retention",
    )(q, k, v, gate, sf0, sb0, decf, decb)


def _fourier_kernel(f_ref, w1_ref, w2_ref, wc_ref, o_ref,
                    xs, ar_s, ai_s, xr_s, xi_s):
    gh = xs.shape[0] // FFT_PITCH
    for t1 in range(gh):
        xs[pl.ds(t1 * FFT_PITCH, GRID_W), :] = f_ref[0, 0, pl.ds(t1 * GRID_W, GRID_W), :].astype(F32)
    for t2 in range(0, GRID_W, 2):
        x = jnp.concatenate([xs[pl.ds(t2 + u, gh, stride=FFT_PITCH), :] for u in range(2)], axis=1)
        r = jnp.dot(w1_ref[...], x.astype(BF16), preferred_element_type=F32)
        for u in range(2):
            lanes = slice(u * GROUP_DIM, (u + 1) * GROUP_DIM)
            ar_s[pl.ds(t2 + u, gh, stride=FFT_PITCH), :] = r[:gh, lanes]
            ai_s[pl.ds(t2 + u, gh, stride=FFT_PITCH), :] = r[gh:, lanes]
    for k1 in range(gh):
        rows = pl.ds(k1 * FFT_PITCH, GRID_W)
        stacked = jnp.concatenate([ar_s[rows, :], ai_s[rows, :]], axis=0).astype(BF16)
        r = jnp.dot(w2_ref[k1], stacked, preferred_element_type=F32)
        xr_s[pl.ds(k1, GRID_W, stride=FFT_PITCH), :] = r[:GRID_W]
        xi_s[pl.ds(k1, GRID_W, stride=FFT_PITCH), :] = r[GRID_W:]
    for kb in range(0, GRID_W, FFT_K2_BATCH):
        parts = []
        for k2 in range(kb, kb + FFT_K2_BATCH):
            rows = pl.ds(k2 * FFT_PITCH, gh)
            parts.append(jnp.concatenate([xr_s[rows, :], xi_s[rows, :]], axis=1).astype(BF16))
        res = jnp.dot(jnp.concatenate(parts, axis=0), wc_ref[...], preferred_element_type=F32)
        o_ref[0, 0, pl.ds(kb * gh, FFT_K2_BATCH * gh), :] = res.astype(BF16)


def _fourier(f, consts):
    bsz, n_groups, seq, _ = f.shape
    gh = seq // GRID_W
    tok = pl.BlockSpec((1, 1, seq, GROUP_DIM), lambda b, g: (b, g, 0, 0))
    const = lambda a: pl.BlockSpec(a.shape, lambda b, g: (0,) * a.ndim)
    names = ("w1", "w2", "wc")
    pitched = pltpu.VMEM((gh * FFT_PITCH, GROUP_DIM), F32)
    return pl.pallas_call(
        _fourier_kernel,
        out_shape=jax.ShapeDtypeStruct(f.shape, BF16),
        grid=(bsz, n_groups),
        in_specs=[tok] + [const(consts[n]) for n in names],
        out_specs=tok,
        scratch_shapes=[pitched] * 5,
        compiler_params=_params("arbitrary", "arbitrary"),
        name="fourier",
    )(f, *[consts[n] for n in names])


def _dft_constants(seq):
    gh = seq // GRID_W
    def cs(n_rows, n_cols, period):
        ang = 2.0 * np.pi * np.outer(np.arange(n_rows), np.arange(n_cols)) / period
        return np.cos(ang), np.sin(ang)
    c1, s1 = cs(gh, gh, gh)
    w1 = np.concatenate([c1, -s1], axis=0) / gh
    c2, s2 = cs(GRID_W, GRID_W, GRID_W)
    cc, sc = cs(GROUP_DIM, GROUP_DIM, GROUP_DIM)
    wc = np.concatenate([cc, sc], axis=0) / math.sqrt(GROUP_DIM)
    twc, tws = cs(gh, GRID_W, seq)
    p = c2[None] * twc[:, None, :] - s2[None] * tws[:, None, :]
    q = s2[None] * twc[:, None, :] + c2[None] * tws[:, None, :]
    w2 = np.concatenate([np.concatenate([p, q], axis=2), np.concatenate([-q, p], axis=2)], axis=1)
    w2 = w2 / math.sqrt(GRID_W * gh) * gh
    t = np.arange(seq)
    n_freq = HEAD_DIM // 4
    freqs = ROPE_BASE ** (-np.arange(n_freq, dtype=np.float32) / n_freq)
    rang = np.concatenate([(t // GRID_W).astype(np.float32)[:, None] * freqs,
                           (t % GRID_W).astype(np.float32)[:, None] * freqs], axis=-1).astype(np.float32)
    cosf = np.concatenate([np.cos(rang), np.cos(rang)], axis=-1)
    sinf = np.concatenate([-np.sin(rang), np.sin(rang)], axis=-1)
    as_f32 = lambda a: jnp.asarray(np.ascontiguousarray(a), F32)
    as_bf16 = lambda a: jnp.asarray(np.ascontiguousarray(a), F32).astype(BF16)
    return dict(w1=as_bf16(w1), w2=as_bf16(w2), wc=as_bf16(wc),
                cosf=as_f32(cosf), sinf=as_f32(sinf))


def _out_ffn_kernel(x_ref, ret_ref, four_ref, mod_ref, gpost_ref, gpre_ref, gfpost_ref,
                    wo_ref, wup_ref, taps_ref, wd_ref, o_ref, h_ref, u_ref, act_ref):
    tm = x_ref.shape[1]
    th = tm // FF_GROUPS
    d_ff = wd_ref.shape[0]
    gt_a, sh_f, sc_f, gt_f = mod_ref[0, 0:1], mod_ref[0, 1:2], mod_ref[0, 2:3], mod_ref[0, 3:4]
    groups = [slice(s * th, (s + 1) * th) for s in range(FF_GROUPS)]
    gain_a = gpost_ref[...] * gt_a
    gain_f = gpre_ref[...] * (1.0 + sc_f)
    gain_o = gfpost_ref[...] * gt_f

    for rows in groups:
        mixed = [ret_ref[0, i, rows, :] for i in range(RET_HEADS)]
        mixed += [four_ref[0, i, rows, :] for i in range(FOURIER_GROUPS)]
        mix = jnp.dot(jnp.concatenate(mixed, axis=1), wo_ref[...], preferred_element_type=F32)
        x1 = x_ref[0, rows, :] + mix * _rms_scale(mix) * gain_a
        o_ref[0, rows, :] = x1
        h_ref[rows, :] = (x1 * _rms_scale(x1) * gain_f + sh_f).astype(BF16)
    u_ref[:, :, 0:U_PAD, :] = jnp.zeros(u_ref.shape[:2] + (U_PAD, LANES), F32)
    u_ref[:, :, U_PAD + tm:, :] = jnp.zeros(u_ref.shape[:2] + (U_PAD, LANES), F32)

    lane_tiles = FF_CHUNK // LANES
    sub = lax.broadcasted_iota(jnp.int32, (SUBLANES, LANES), 0)
    not_first = (sub != 0).astype(F32)
    not_last = (sub != SUBLANES - 1).astype(F32)

    def conv(slot, lt, lo):
        w_prev, w_cur, w_next, bias = (taps_ref[i:i + 1, lo:lo + LANES] for i in range(4))
        blocks = []
        for r in range(0, tm, GRID_W):
            prev = u_ref[slot, lt, U_PAD - 1 + r:U_PAD - 1 + r + GRID_W, :]
            nxt = u_ref[slot, lt, U_PAD + 1 + r:U_PAD + 1 + r + GRID_W, :]
            cur = u_ref[slot, lt, U_PAD + r:U_PAD + r + GRID_W, :]
            prev = jnp.concatenate([prev[:SUBLANES] * not_first, prev[SUBLANES:]], axis=0)
            nxt = jnp.concatenate([nxt[:GRID_W - SUBLANES], nxt[GRID_W - SUBLANES:] * not_last], axis=0)
            blocks.append(prev * w_prev + nxt * w_next + cur * w_cur + bias)
        return jnp.concatenate(blocks, axis=0)

    for j in range(d_ff // FF_CHUNK):
        cols = (j * FF_CHUNK, d_ff + j * FF_CHUNK)
        slots = (2 * j, 2 * j + 1)
        h = h_ref[...]
        for slot, lo in zip(slots, cols):
            u = jnp.dot(h, wup_ref[:, lo:lo + FF_CHUNK], preferred_element_type=F32)
            for lt in range(lane_tiles):
                u_ref[slot, lt, U_PAD:U_PAD + tm, :] = u[:, lt * LANES:(lt + 1) * LANES]
        for lt in range(lane_tiles):
            a = conv(slots[0], lt, cols[0] + lt * LANES)
            b = conv(slots[1], lt, cols[1] + lt * LANES)
            lo = j * FF_CHUNK + lt * LANES
            act_ref[:, lo:lo + LANES] = (_silu(a) * b).astype(BF16)

    for rows in groups:
        ffn = jnp.dot(act_ref[rows, :], wd_ref[...], preferred_element_type=F32)
        o_ref[0, rows, :] = o_ref[0, rows, :] + ffn * _rms_scale(ffn) * gain_o


def _out_ffn(x, ret, four, mod, gpost, gpre, gfpost, wo, wup, taps, wd, tm):
    bsz, seq, d = x.shape
    d_ff = wd.shape[0]
    tok = lambda w: pl.BlockSpec((1, tm, w), lambda b, i: (b, i, 0))
    split = lambda n: pl.BlockSpec((1, n, tm, LANES), lambda b, i: (b, 0, i, 0))
    vec = pl.BlockSpec((1, d), lambda b, i: (0, 0))
    const = lambda a: pl.BlockSpec(a.shape, lambda b, i: (0,) * a.ndim, pipeline_mode=pl.Buffered(1))
    return pl.pallas_call(
        _out_ffn_kernel,
        out_shape=jax.ShapeDtypeStruct((bsz, seq, d), F32),
        grid=(bsz, seq // tm),
        in_specs=[tok(d), split(RET_HEADS), split(FOURIER_GROUPS),
                  pl.BlockSpec((1, 4, d), lambda b, i: (b, 0, 0)), vec, vec, vec,
                  const(wo), const(wup), const(taps), const(wd)],
        out_specs=tok(d),
        scratch_shapes=[pltpu.VMEM((tm, d), BF16),
                        pltpu.VMEM((2 * (d_ff // FF_CHUNK), FF_CHUNK // LANES, tm + 2 * U_PAD, LANES), F32),
                        pltpu.VMEM((tm, d_ff), BF16)],
        compiler_params=_params("arbitrary", "arbitrary"),
        name="out_ffn",
    )(x, ret, four, mod, gpost, gpre, gfpost, wo, wup, taps, wd)


def _layer(x, ctx, cond, consts, w_ada, b_ada, g_mix_pre, g_mix_post, g_ffn_pre, g_ffn_post,
           w_in, dec_f, dec_b, w_out, w_up, conv_w, conv_b, w_down):
    bsz, seq, d = x.shape
    d_ff = w_down.shape[0]
    row = lambda v: v.reshape(1, -1)

    mod = _adaln(cond, w_ada, row(b_ada))
    mod = mod.reshape(COND_ROWS, N_MOD, d)
    lat = mod[:bsz]
    cmod = mod[bsz]

    decf = jnp.broadcast_to(dec_f[:, None], (RET_HEADS, HEAD_DIM))
    decb = jnp.broadcast_to(dec_b[:, None], (RET_HEADS, HEAD_DIM))
    w_in_bf = w_in.astype(BF16)
    sf0, sb0 = _ctx_states(ctx, cmod[0:1], cmod[1:2], row(g_mix_pre),
                           w_in_bf[:, RET_W:3 * RET_W], decf, decb)

    q, k, v, gate, f = _in_proj(x, lat[:, 0:1], lat[:, 1:2], row(g_mix_pre), w_in_bf,
                                consts["cosf"], consts["sinf"], tm=1024)
    ret = _retention(q, k, v, gate, sf0, sb0, decf, decb)
    four = _fourier(f, consts)

    taps = jnp.concatenate([conv_w, conv_b[None], jnp.zeros((SUBLANES - CONV_WIDTH - 1, 2 * d_ff), F32)], axis=0)
    return _out_ffn(x, ret, four, lat[:, 2:6], row(g_mix_post), row(g_ffn_pre), row(g_ffn_post),
                    w_out.astype(BF16), w_up.astype(BF16), taps, w_down.astype(BF16), tm=512)


def kernel(x, c, ctx, c_ctx, w_ada, b_ada, g_mix_pre, g_mix_post, g_ffn_pre, g_ffn_post, w_in,
           ret_decay_fwd, ret_decay_bwd, w_out, w_up, conv_w, conv_b, w_down):
    bsz, seq, d = x.shape
    depth = w_ada.shape[0]
    assert depth == 1, "the context stream update of deeper stacks is not implemented"
    assert bsz + 1 <= COND_ROWS and seq % (GRID_W * GRID_W) == 0 and seq // GRID_W == GRID_W
    consts = _dft_constants(seq)
    cond = jnp.concatenate([c, c_ctx[None], jnp.zeros((COND_ROWS - bsz - 1, d), F32)], axis=0)
    layer = 0
    return _layer(x, ctx, cond, consts, w_ada[layer], b_ada[layer], g_mix_pre[layer], g_mix_post[layer],
                  g_ffn_pre[layer], g_ffn_post[layer], w_in[layer], ret_decay_fwd[layer],
                  ret_decay_bwd[layer], w_out[layer], w_up[layer], conv_w[layer], conv_b[layer],
                  w_down[layer])
```

```python
import math

import numpy as np
import jax
import jax.numpy as jnp
from jax import lax
from jax.experimental import pallas as pl
from jax.experimental.pallas import tpu as pltpu

F32 = jnp.float32
BF16 = jnp.bfloat16

RET_HEADS = 4
HEAD_DIM = 128
RET_W = RET_HEADS * HEAD_DIM
FOURIER_GROUPS = 4
GROUP_DIM = 128
LANES = 128
SUBLANES = 8
GRID_W = 64
CHUNK = 256
ROPE_BASE = 10000.0
NORM_EPS = 1e-6
N_MOD = 6
CONV_WIDTH = 3
CTX_BATCH = 4
COND_ROWS = 24
FF_CHUNK = 256
RET_UNROLL = 16
IN_GROUPS = 2
FF_GROUPS = 2
U_PAD = SUBLANES
FFT_PITCH = GRID_W + 4
FFT_K2_BATCH = 8
VMEM_LIMIT = 56 * 1024 * 1024


def _silu(x):
    return x / (1.0 + jnp.exp(-x))


def _rms_scale(x):
    return lax.rsqrt(jnp.mean(x * x, axis=-1, keepdims=True) + NORM_EPS)


def _rms_norm(x, g):
    return x * _rms_scale(x) * g


def _log_sigmoid(x):
    return jnp.minimum(x, 0.0) - jnp.log(1.0 + jnp.exp(-jnp.abs(x)))


def _params(*sem):
    return pltpu.CompilerParams(dimension_semantics=sem, vmem_limit_bytes=VMEM_LIMIT)


def _adaln_kernel(cond_ref, w_ref, b_ref, o_ref):
    s = _silu(cond_ref[...])
    o_ref[...] = jnp.dot(s, w_ref[...], preferred_element_type=F32) + b_ref[...]


def _adaln(cond, w, b):
    rows, d = cond.shape
    n = w.shape[1]
    tn = d
    return pl.pallas_call(
        _adaln_kernel,
        out_shape=jax.ShapeDtypeStruct((rows, n), F32),
        grid=(n // tn,),
        in_specs=[pl.BlockSpec((rows, d), lambda j: (0, 0)),
                  pl.BlockSpec((d, tn), lambda j: (0, j)),
                  pl.BlockSpec((1, tn), lambda j: (0, j))],
        out_specs=pl.BlockSpec((rows, tn), lambda j: (0, j)),
        compiler_params=_params("arbitrary"),
        name="adaln",
    )(cond, w, b)


def _ctx_kernel(ctx_ref, sh_ref, sc_ref, g_ref, wk_ref, wv_ref, decf_ref, decb_ref, sf_ref, sb_ref):
    n_batch, n_ctx, d = ctx_ref.shape
    x = ctx_ref[...].reshape(n_batch * n_ctx, d)
    h = (_rms_norm(x, g_ref[...]) * (1.0 + sc_ref[...]) + sh_ref[...]).astype(BF16)
    k_all = jnp.dot(h, wk_ref[...], preferred_element_type=F32) * (HEAD_DIM ** -0.5)
    v_all = jnp.dot(h, wv_ref[...], preferred_element_type=F32).astype(BF16)
    lgf = _log_sigmoid(decf_ref[...])
    lgb = _log_sigmoid(decb_ref[...])
    m = lax.broadcasted_iota(jnp.int32, (n_ctx, HEAD_DIM), 0).astype(F32)
    dn = (((0,), (0,)), ((), ()))
    for hd in range(RET_HEADS):
        lanes = slice(hd * HEAD_DIM, (hd + 1) * HEAD_DIM)
        w_f = jnp.exp((n_ctx - 1.0 - m) * lgf[hd:hd + 1, :])
        w_b = jnp.exp(m * lgb[hd:hd + 1, :])
        for b in range(n_batch):
            rows = slice(b * n_ctx, (b + 1) * n_ctx)
            k, v = k_all[rows, lanes], v_all[rows, lanes]
            sf_ref[b, hd] = lax.dot_general((k * w_f).astype(BF16), v, dn, preferred_element_type=F32)
            sb_ref[b, hd] = lax.dot_general((k * w_b).astype(BF16), v, dn, preferred_element_type=F32)


def _ctx_states(ctx, sh, sc, g, w_in, decf, decb):
    bsz, n_ctx, d = ctx.shape
    st = jax.ShapeDtypeStruct((bsz, RET_HEADS, HEAD_DIM, HEAD_DIM), F32)
    st_spec = pl.BlockSpec((CTX_BATCH, RET_HEADS, HEAD_DIM, HEAD_DIM), lambda i: (i, 0, 0, 0))
    vec = pl.BlockSpec((1, d), lambda i: (0, 0))
    dec = pl.BlockSpec((RET_HEADS, HEAD_DIM), lambda i: (0, 0))
    w_cols = lambda j: pl.BlockSpec((d, RET_W), lambda i: (0, j))
    return pl.pallas_call(
        _ctx_kernel,
        out_shape=(st, st),
        grid=(bsz // CTX_BATCH,),
        in_specs=[pl.BlockSpec((CTX_BATCH, n_ctx, d), lambda i: (i, 0, 0)), vec, vec, vec,
                  w_cols(1), w_cols(2), dec, dec],
        out_specs=(st_spec, st_spec),
        compiler_params=_params("arbitrary"),
        name="ctx_state",
    )(ctx, sh, sc, g, w_in, w_in, decf, decb)


def _in_proj_kernel(x_ref, sh_ref, sc_ref, g_ref, w_ref, cos_ref, sin_ref,
                    q_ref, k_ref, v_ref, gate_ref, f_ref):
    th = x_ref.shape[1] // IN_GROUPS
    gain = g_ref[...] * (1.0 + sc_ref[0])
    for s in range(IN_GROUPS):
        rows = slice(s * th, (s + 1) * th)
        x = x_ref[0, rows, :]
        h = x * _rms_scale(x) * gain + sh_ref[0]
        proj = jnp.dot(h.astype(BF16), w_ref[...], preferred_element_type=F32)
        cosf = cos_ref[rows, :]
        sinf = sin_ref[rows, :]
        for hd in range(RET_HEADS):
            lo = hd * HEAD_DIM
            q = proj[:, lo:lo + HEAD_DIM]
            k = proj[:, RET_W + lo:RET_W + lo + HEAD_DIM] * (HEAD_DIM ** -0.5)
            q_ref[0, hd, rows, :] = (q * cosf + pltpu.roll(q, HEAD_DIM // 2, 1) * sinf).astype(BF16)
            k_ref[0, hd, rows, :] = (k * cosf + pltpu.roll(k, HEAD_DIM // 2, 1) * sinf).astype(BF16)
            v_ref[0, hd, rows, :] = proj[:, 2 * RET_W + lo:2 * RET_W + lo + HEAD_DIM].astype(BF16)
            gate_ref[0, hd, rows, :] = proj[:, 3 * RET_W + lo:3 * RET_W + lo + HEAD_DIM].astype(BF16)
        for g in range(FOURIER_GROUPS):
            lo = 4 * RET_W + g * GROUP_DIM
            f_ref[0, g, rows, :] = proj[:, lo:lo + GROUP_DIM].astype(BF16)


def _in_proj(x, sh, sc, g, w_in, cosf, sinf, tm):
    bsz, seq, d = x.shape
    mod = pl.BlockSpec((1, 1, d), lambda b, i: (b, 0, 0))
    split = jax.ShapeDtypeStruct((bsz, RET_HEADS, seq, HEAD_DIM), BF16)
    split_spec = pl.BlockSpec((1, RET_HEADS, tm, HEAD_DIM), lambda b, i: (b, 0, i, 0))
    rope = pl.BlockSpec((tm, HEAD_DIM), lambda b, i: (i, 0))
    return pl.pallas_call(
        _in_proj_kernel,
        out_shape=(split,) * 5,
        grid=(bsz, seq // tm),
        in_specs=[pl.BlockSpec((1, tm, d), lambda b, i: (b, i, 0)), mod, mod,
                  pl.BlockSpec((1, d), lambda b, i: (0, 0)),
                  pl.BlockSpec(w_in.shape, lambda b, i: (0, 0)), rope, rope],
        out_specs=(split_spec,) * 5,
        compiler_params=_params("arbitrary", "arbitrary"),
        name="in_proj",
    )(x, sh, sc, g, w_in, cosf, sinf)


def _retention_kernel(q_ref, k_ref, v_ref, gate_ref, sf0_ref, sb0_ref, decf_ref, decb_ref,
                      o_ref, kvf_scr, kvb_scr, sf_scr, sb_scr):
    hd = pl.program_id(1)
    n_chunks = q_ref.shape[2] // CHUNK
    lgf = _log_sigmoid(decf_ref[pl.ds(hd, 1), :])
    lgb = _log_sigmoid(decb_ref[pl.ds(hd, 1), :])
    pos = lax.broadcasted_iota(jnp.int32, (CHUNK, HEAD_DIM), 0).astype(F32)
    diff = (lax.broadcasted_iota(jnp.int32, (CHUNK, CHUNK), 0)
            - lax.broadcasted_iota(jnp.int32, (CHUNK, CHUNK), 1)).astype(F32)
    lgf_c = jnp.concatenate([lgf] * (CHUNK // HEAD_DIM), axis=1)
    lgb_c = jnp.concatenate([lgb] * (CHUNK // HEAD_DIM), axis=1)
    zeta_f = jnp.exp((CHUNK - 1.0 - pos) * lgf)
    zeta_b = jnp.exp(pos * lgb)
    xi_f = jnp.exp((pos + 1.0) * lgf)
    xi_b = jnp.exp((CHUNK - pos) * lgb)
    decay = jnp.exp(jnp.where(diff >= 0, diff * lgf_c, -diff * lgb_c))
    cdec_f = jnp.exp(CHUNK * lgf)
    cdec_b = jnp.exp(CHUNK * lgb)
    tn = (((0,), (0,)), ((), ()))
    nt = (((1,), (1,)), ((), ()))

    def chunk_kv(n, carry):
        rows = pl.ds(pl.multiple_of(n * CHUNK, CHUNK), CHUNK)
        k = k_ref[0, 0, rows, :].astype(F32)
        v = v_ref[0, 0, rows, :]
        kvf_scr[n] = lax.dot_general((k * zeta_f).astype(BF16), v, tn, preferred_element_type=F32)
        kvb_scr[n] = lax.dot_general((k * zeta_b).astype(BF16), v, tn, preferred_element_type=F32)
        return carry

    lax.fori_loop(0, n_chunks, chunk_kv, 0, unroll=RET_UNROLL)

    def scan_f(n, state):
        sf_scr[n] = state.astype(BF16)
        return state * cdec_f + kvf_scr[n]

    lax.fori_loop(0, n_chunks, scan_f, sf0_ref[0, 0])

    def scan_b(i, state):
        n = n_chunks - 1 - i
        sb_scr[n] = state.astype(BF16)
        return state * cdec_b + kvb_scr[n]

    lax.fori_loop(0, n_chunks, scan_b, sb0_ref[0, 0])

    def chunk_out(n, carry):
        rows = pl.ds(pl.multiple_of(n * CHUNK, CHUNK), CHUNK)
        q = q_ref[0, 0, rows, :]
        k = k_ref[0, 0, rows, :]
        v = v_ref[0, 0, rows, :]
        scores = lax.dot_general(q, k, nt, preferred_element_type=F32) * decay
        qf = q.astype(F32)
        o = jnp.dot(scores.astype(BF16), v, preferred_element_type=F32)
        o = o + jnp.dot((qf * xi_f).astype(BF16), sf_scr[n], preferred_element_type=F32)
        o = o + jnp.dot((qf * xi_b).astype(BF16), sb_scr[n], preferred_element_type=F32)
        o = o * lax.rsqrt(jnp.mean(o * o, axis=-1, keepdims=True) + NORM_EPS)
        o_ref[0, 0, rows, :] = (o * _silu(gate_ref[0, 0, rows, :].astype(F32))).astype(BF16)
        return carry

    lax.fori_loop(0, n_chunks, chunk_out, 0, unroll=RET_UNROLL)


def _retention(q, k, v, gate, sf0, sb0, decf, decb):
    bsz, _, seq, _ = q.shape
    n_chunks = seq // CHUNK
    tok = pl.BlockSpec((1, 1, seq, HEAD_DIM), lambda b, h: (b, h, 0, 0))
    st = pl.BlockSpec((1, 1, HEAD_DIM, HEAD_DIM), lambda b, h: (b, h, 0, 0))
    dec = pl.BlockSpec((RET_HEADS, HEAD_DIM), lambda b, h: (0, 0))
    return pl.pallas_call(
        _retention_kernel,
        out_shape=jax.ShapeDtypeStruct(q.shape, BF16),
        grid=(bsz, RET_HEADS),
        in_specs=[tok, tok, tok, tok, st, st, dec, dec],
        out_specs=tok,
        scratch_shapes=[pltpu.VMEM((n_chunks, HEAD_DIM, HEAD_DIM), F32),
                        pltpu.VMEM((n_chunks, HEAD_DIM, HEAD_DIM), F32),
                        pltpu.VMEM((n_chunks, HEAD_DIM, HEAD_DIM), BF16),
                        pltpu.VMEM((n_chunks, HEAD_DIM, HEAD_DIM), BF16)],
        compiler_params=_params("arbitrary", "arbitrary"),
        name="retention",
    )(q, k, v, gate, sf0, sb0, decf, decb)


def _fourier_kernel(f_ref, w1_ref, w2_ref, wc_ref, o_ref,
                    xs, ar_s, ai_s, xr_s, xi_s):
    gh = xs.shape[0] // FFT_PITCH
    for t1 in range(gh):
        xs[pl.ds(t1 * FFT_PITCH, GRID_W), :] = f_ref[0, 0, pl.ds(t1 * GRID_W, GRID_W), :].astype(F32)
    for t2 in range(0, GRID_W, 2):
        x = jnp.concatenate([xs[pl.ds(t2 + u, gh, stride=FFT_PITCH), :] for u in range(2)], axis=1)
        r = jnp.dot(w1_ref[...], x.astype(BF16), preferred_element_type=F32)
        for u in range(2):
            lanes = slice(u * GROUP_DIM, (u + 1) * GROUP_DIM)
            ar_s[pl.ds(t2 + u, gh, stride=FFT_PITCH), :] = r[:gh, lanes]
            ai_s[pl.ds(t2 + u, gh, stride=FFT_PITCH), :] = r[gh:, lanes]
    for k1 in range(gh):
        rows = pl.ds(k1 * FFT_PITCH, GRID_W)
        stacked = jnp.concatenate([ar_s[rows, :], ai_s[rows, :]], axis=0).astype(BF16)
        r = jnp.dot(w2_ref[k1], stacked, preferred_element_type=F32)
        xr_s[pl.ds(k1, GRID_W, stride=FFT_PITCH), :] = r[:GRID_W]
        xi_s[pl.ds(k1, GRID_W, stride=FFT_PITCH), :] = r[GRID_W:]
    for kb in range(0, GRID_W, FFT_K2_BATCH):
        parts = []
        for k2 in range(kb, kb + FFT_K2_BATCH):
            rows = pl.ds(k2 * FFT_PITCH, gh)
            parts.append(jnp.concatenate([xr_s[rows, :], xi_s[rows, :]], axis=1).astype(BF16))
        res = jnp.dot(jnp.concatenate(parts, axis=0), wc_ref[...], preferred_element_type=F32)
        o_ref[0, 0, pl.ds(kb * gh, FFT_K2_BATCH * gh), :] = res.astype(BF16)


def _fourier(f, consts):
    bsz, n_groups, seq, _ = f.shape
    gh = seq // GRID_W
    tok = pl.BlockSpec((1, 1, seq, GROUP_DIM), lambda b, g: (b, g, 0, 0))
    const = lambda a: pl.BlockSpec(a.shape, lambda b, g: (0,) * a.ndim)
    names = ("w1", "w2", "wc")
    pitched = pltpu.VMEM((gh * FFT_PITCH, GROUP_DIM), F32)
    return pl.pallas_call(
        _fourier_kernel,
        out_shape=jax.ShapeDtypeStruct(f.shape, BF16),
        grid=(bsz, n_groups),
        in_specs=[tok] + [const(consts[n]) for n in names],
        out_specs=tok,
        scratch_shapes=[pitched] * 5,
        compiler_params=_params("arbitrary", "arbitrary"),
        name="fourier",
    )(f, *[consts[n] for n in names])


def _dft_constants(seq):
    gh = seq // GRID_W
    def cs(n_rows, n_cols, period):
        ang = 2.0 * np.pi * np.outer(np.arange(n_rows), np.arange(n_cols)) / period
        return np.cos(ang), np.sin(ang)
    c1, s1 = cs(gh, gh, gh)
    w1 = np.concatenate([c1, -s1], axis=0) / gh
    c2, s2 = cs(GRID_W, GRID_W, GRID_W)
    cc, sc = cs(GROUP_DIM, GROUP_DIM, GROUP_DIM)
    wc = np.concatenate([cc, sc], axis=0) / math.sqrt(GROUP_DIM)
    twc, tws = cs(gh, GRID_W, seq)
    p = c2[None] * twc[:, None, :] - s2[None] * tws[:, None, :]
    q = s2[None] * twc[:, None, :] + c2[None] * tws[:, None, :]
    w2 = np.concatenate([np.concatenate([p, q], axis=2), np.concatenate([-q, p], axis=2)], axis=1)
    w2 = w2 / math.sqrt(GRID_W * gh) * gh
    t = np.arange(seq)
    n_freq = HEAD_DIM // 4
    freqs = ROPE_BASE ** (-np.arange(n_freq, dtype=np.float32) / n_freq)
    rang = np.concatenate([(t // GRID_W).astype(np.float32)[:, None] * freqs,
                           (t % GRID_W).astype(np.float32)[:, None] * freqs], axis=-1).astype(np.float32)
    cosf = np.concatenate([np.cos(rang), np.cos(rang)], axis=-1)
    sinf = np.concatenate([-np.sin(rang), np.sin(rang)], axis=-1)
    as_f32 = lambda a: jnp.asarray(np.ascontiguousarray(a), F32)
    as_bf16 = lambda a: jnp.asarray(np.ascontiguousarray(a), F32).astype(BF16)
    return dict(w1=as_bf16(w1), w2=as_bf16(w2), wc=as_bf16(wc),
                cosf=as_f32(cosf), sinf=as_f32(sinf))


def _out_ffn_kernel(x_ref, ret_ref, four_ref, mod_ref, gpost_ref, gpre_ref, gfpost_ref,
                    wo_ref, wup_ref, taps_ref, wd_ref, o_ref, h_ref, u_ref, act_ref):
    tm = x_ref.shape[1]
    th = tm // FF_GROUPS
    d_ff = wd_ref.shape[0]
    gt_a, sh_f, sc_f, gt_f = mod_ref[0, 0:1], mod_ref[0, 1:2], mod_ref[0, 2:3], mod_ref[0, 3:4]
    groups = [slice(s * th, (s + 1) * th) for s in range(FF_GROUPS)]
    gain_a = gpost_ref[...] * gt_a
    gain_f = gpre_ref[...] * (1.0 + sc_f)
    gain_o = gfpost_ref[...] * gt_f

    for rows in groups:
        mixed = [ret_ref[0, i, rows, :] for i in range(RET_HEADS)]
        mixed += [four_ref[0, i, rows, :] for i in range(FOURIER_GROUPS)]
        mix = jnp.dot(jnp.concatenate(mixed, axis=1), wo_ref[...], preferred_element_type=F32)
        x1 = x_ref[0, rows, :] + mix * _rms_scale(mix) * gain_a
        o_ref[0, rows, :] = x1
        h_ref[rows, :] = (x1 * _rms_scale(x1) * gain_f + sh_f).astype(BF16)
    u_ref[:, :, 0:U_PAD, :] = jnp.zeros(u_ref.shape[:2] + (U_PAD, LANES), F32)
    u_ref[:, :, U_PAD + tm:, :] = jnp.zeros(u_ref.shape[:2] + (U_PAD, LANES), F32)

    lane_tiles = FF_CHUNK // LANES
    sub = lax.broadcasted_iota(jnp.int32, (SUBLANES, LANES), 0)
    not_first = (sub != 0).astype(F32)
    not_last = (sub != SUBLANES - 1).astype(F32)

    def conv(slot, lt, lo):
        w_prev, w_cur, w_next, bias = (taps_ref[i:i + 1, lo:lo + LANES] for i in range(4))
        blocks = []
        for r in range(0, tm, GRID_W):
            prev = u_ref[slot, lt, U_PAD - 1 + r:U_PAD - 1 + r + GRID_W, :]
            nxt = u_ref[slot, lt, U_PAD + 1 + r:U_PAD + 1 + r + GRID_W, :]
            cur = u_ref[slot, lt, U_PAD + r:U_PAD + r + GRID_W, :]
            prev = jnp.concatenate([prev[:SUBLANES] * not_first, prev[SUBLANES:]], axis=0)
            nxt = jnp.concatenate([nxt[:GRID_W - SUBLANES], nxt[GRID_W - SUBLANES:] * not_last], axis=0)
            blocks.append(prev * w_prev + nxt * w_next + cur * w_cur + bias)
        return jnp.concatenate(blocks, axis=0)

    for j in range(d_ff // FF_CHUNK):
        cols = (j * FF_CHUNK, d_ff + j * FF_CHUNK)
        slots = (2 * j, 2 * j + 1)
        h = h_ref[...]
        for slot, lo in zip(slots, cols):
            u = jnp.dot(h, wup_ref[:, lo:lo + FF_CHUNK], preferred_element_type=F32)
            for lt in range(lane_tiles):
                u_ref[slot, lt, U_PAD:U_PAD + tm, :] = u[:, lt * LANES:(lt + 1) * LANES]
        for lt in range(lane_tiles):
            a = conv(slots[0], lt, cols[0] + lt * LANES)
            b = conv(slots[1], lt, cols[1] + lt * LANES)
            lo = j * FF_CHUNK + lt * LANES
            act_ref[:, lo:lo + LANES] = (_silu(a) * b).astype(BF16)

    for rows in groups:
        ffn = jnp.dot(act_ref[rows, :], wd_ref[...], preferred_element_type=F32)
        o_ref[0, rows, :] = o_ref[0, rows, :] + ffn * _rms_scale(ffn) * gain_o


def _out_ffn(x, ret, four, mod, gpost, gpre, gfpost, wo, wup, taps, wd, tm):
    bsz, seq, d = x.shape
    d_ff = wd.shape[0]
    tok = lambda w: pl.BlockSpec((1, tm, w), lambda b, i: (b, i, 0))
    split = lambda n: pl.BlockSpec((1, n, tm, LANES), lambda b, i: (b, 0, i, 0))
    vec = pl.BlockSpec((1, d), lambda b, i: (0, 0))
    const = lambda a: pl.BlockSpec(a.shape, lambda b, i: (0,) * a.ndim, pipeline_mode=pl.Buffered(1))
    return pl.pallas_call(
        _out_ffn_kernel,
        out_shape=jax.ShapeDtypeStruct((bsz, seq, d), F32),
        grid=(bsz, seq // tm),
        in_specs=[tok(d), split(RET_HEADS), split(FOURIER_GROUPS),
                  pl.BlockSpec((1, 4, d), lambda b, i: (b, 0, 0)), vec, vec, vec,
                  const(wo), const(wup), const(taps), const(wd)],
        out_specs=tok(d),
        scratch_shapes=[pltpu.VMEM((tm, d), BF16),
                        pltpu.VMEM((2 * (d_ff // FF_CHUNK), FF_CHUNK // LANES, tm + 2 * U_PAD, LANES), F32),
                        pltpu.VMEM((tm, d_ff), BF16)],
        compiler_params=_params("arbitrary", "arbitrary"),
        name="out_ffn",
    )(x, ret, four, mod, gpost, gpre, gfpost, wo, wup, taps, wd)


def _layer(x, ctx, cond, consts, w_ada, b_ada, g_mix_pre, g_mix_post, g_ffn_pre, g_ffn_post,
           w_in, dec_f, dec_b, w_out, w_up, conv_w, conv_b, w_down):
    bsz, seq, d = x.shape
    d_ff = w_down.shape[0]
    row = lambda v: v.reshape(1, -1)

    mod = _adaln(cond, w_ada, row(b_ada))
    mod = mod.reshape(COND_ROWS, N_MOD, d)
    lat = mod[:bsz]
    cmod = mod[bsz]

    decf = jnp.broadcast_to(dec_f[:, None], (RET_HEADS, HEAD_DIM))
    decb = jnp.broadcast_to(dec_b[:, None], (RET_HEADS, HEAD_DIM))
    w_in_bf = w_in.astype(BF16)
    sf0, sb0 = _ctx_states(ctx, cmod[0:1], cmod[1:2], row(g_mix_pre), w_in_bf, decf, decb)

    q, k, v, gate, f = _in_proj(x, lat[:, 0:1], lat[:, 1:2], row(g_mix_pre), w_in_bf,
                                consts["cosf"], consts["sinf"], tm=1024)
    ret = _retention(q, k, v, gate, sf0, sb0, decf, decb)
    four = _fourier(f, consts)

    taps = jnp.concatenate([conv_w, conv_b[None], jnp.zeros((SUBLANES - CONV_WIDTH - 1, 2 * d_ff), F32)], axis=0)
    return _out_ffn(x, ret, four, lat[:, 2:6], row(g_mix_post), row(g_ffn_pre), row(g_ffn_post),
                    w_out.astype(BF16), w_up.astype(BF16), taps, w_down.astype(BF16), tm=512)


def kernel(x, c, ctx, c_ctx, w_ada, b_ada, g_mix_pre, g_mix_post, g_ffn_pre, g_ffn_post, w_in,
           ret_decay_fwd, ret_decay_bwd, w_out, w_up, conv_w, conv_b, w_down):
    bsz, seq, d = x.shape
    depth = w_ada.shape[0]
    assert depth == 1, "the context stream update of deeper stacks is not implemented"
    assert bsz + 1 <= COND_ROWS and seq % (GRID_W * GRID_W) == 0 and seq // GRID_W == GRID_W
    assert bsz % CTX_BATCH == 0
    consts = _dft_constants(seq)
    cond = jnp.concatenate([c, c_ctx[None], jnp.zeros((COND_ROWS - bsz - 1, d), F32)], axis=0)
    layer = 0
    return _layer(x, ctx, cond, consts, w_ada[layer], b_ada[layer], g_mix_pre[layer], g_mix_post[layer],
                  g_ffn_pre[layer], g_ffn_post[layer], w_in[layer], ret_decay_fwd[layer],
                  ret_decay_bwd[layer], w_out[layer], w_up[layer], conv_w[layer], conv_b[layer],
                  w_down[layer])
```

```python
import math

import numpy as np
import jax
import jax.numpy as jnp
from jax import lax
from jax.experimental import pallas as pl
from jax.experimental.pallas import tpu as pltpu

F32 = jnp.float32
BF16 = jnp.bfloat16

RET_HEADS = 4
HEAD_DIM = 128
RET_W = RET_HEADS * HEAD_DIM
FOURIER_GROUPS = 4
GROUP_DIM = 128
LANES = 128
SUBLANES = 8
GRID_W = 64
CHUNK = 256
ROPE_BASE = 10000.0
NORM_EPS = 1e-6
N_MOD = 6
CONV_WIDTH = 3
CTX_BATCH = 4
COND_ROWS = 24
FF_CHUNK = 256
RET_UNROLL = 16
IN_GROUPS = 2
FF_GROUPS = 2
U_PAD = SUBLANES
FFT_PITCH = GRID_W + 4
FFT_K2_BATCH = 8
VMEM_LIMIT = 56 * 1024 * 1024


def _silu(x):
    return x / (1.0 + jnp.exp(-x))


def _rms_scale(x):
    return lax.rsqrt(jnp.mean(x * x, axis=-1, keepdims=True) + NORM_EPS)


def _rms_norm(x, g):
    return x * _rms_scale(x) * g


def _log_sigmoid(x):
    return jnp.minimum(x, 0.0) - jnp.log(1.0 + jnp.exp(-jnp.abs(x)))


def _params(*sem):
    return pltpu.CompilerParams(dimension_semantics=sem, vmem_limit_bytes=VMEM_LIMIT)


def _adaln_kernel(cond_ref, w_ref, b_ref, o_ref):
    s = _silu(cond_ref[...])
    o_ref[...] = jnp.dot(s, w_ref[...], preferred_element_type=F32) + b_ref[...]


def _adaln(cond, w, b):
    rows, d = cond.shape
    n = w.shape[1]
    tn = d
    return pl.pallas_call(
        _adaln_kernel,
        out_shape=jax.ShapeDtypeStruct((rows, n), F32),
        grid=(n // tn,),
        in_specs=[pl.BlockSpec((rows, d), lambda j: (0, 0)),
                  pl.BlockSpec((d, tn), lambda j: (0, j)),
                  pl.BlockSpec((1, tn), lambda j: (0, j))],
        out_specs=pl.BlockSpec((rows, tn), lambda j: (0, j)),
        compiler_params=_params("arbitrary"),
        name="adaln",
    )(cond, w, b)


def _ctx_kernel(ctx_ref, sh_ref, sc_ref, g_ref, wk_ref, wv_ref, decf_ref, decb_ref, sf_ref, sb_ref):
    n_batch, n_ctx, d = ctx_ref.shape
    x = ctx_ref[...].reshape(n_batch * n_ctx, d)
    h = (_rms_norm(x, g_ref[...]) * (1.0 + sc_ref[...]) + sh_ref[...]).astype(BF16)
    k_all = jnp.dot(h, wk_ref[...], preferred_element_type=F32) * (HEAD_DIM ** -0.5)
    v_all = jnp.dot(h, wv_ref[...], preferred_element_type=F32).astype(BF16)
    lgf = _log_sigmoid(decf_ref[...])
    lgb = _log_sigmoid(decb_ref[...])
    m = lax.broadcasted_iota(jnp.int32, (n_ctx, HEAD_DIM), 0).astype(F32)
    dn = (((0,), (0,)), ((), ()))
    for hd in range(RET_HEADS):
        lanes = slice(hd * HEAD_DIM, (hd + 1) * HEAD_DIM)
        w_f = jnp.exp((n_ctx - 1.0 - m) * lgf[hd:hd + 1, :])
        w_b = jnp.exp(m * lgb[hd:hd + 1, :])
        for b in range(n_batch):
            rows = slice(b * n_ctx, (b + 1) * n_ctx)
            k, v = k_all[rows, lanes], v_all[rows, lanes]
            sf_ref[b, hd] = lax.dot_general((k * w_f).astype(BF16), v, dn, preferred_element_type=F32)
            sb_ref[b, hd] = lax.dot_general((k * w_b).astype(BF16), v, dn, preferred_element_type=F32)


def _ctx_states(ctx, sh, sc, g, w_in, decf, decb):
    bsz, n_ctx, d = ctx.shape
    st = jax.ShapeDtypeStruct((bsz, RET_HEADS, HEAD_DIM, HEAD_DIM), F32)
    st_spec = pl.BlockSpec((CTX_BATCH, RET_HEADS, HEAD_DIM, HEAD_DIM), lambda i: (i, 0, 0, 0))
    vec = pl.BlockSpec((1, d), lambda i: (0, 0))
    dec = pl.BlockSpec((RET_HEADS, HEAD_DIM), lambda i: (0, 0))
    w_cols = lambda j: pl.BlockSpec((d, RET_W), lambda i: (0, j))
    return pl.pallas_call(
        _ctx_kernel,
        out_shape=(st, st),
        grid=(bsz // CTX_BATCH,),
        in_specs=[pl.BlockSpec((CTX_BATCH, n_ctx, d), lambda i: (i, 0, 0)), vec, vec, vec,
                  w_cols(1), w_cols(2), dec, dec],
        out_specs=(st_spec, st_spec),
        compiler_params=_params("arbitrary"),
        name="ctx_state",
    )(ctx, sh, sc, g, w_in, w_in, decf, decb)


def _in_proj_kernel(x_ref, sh_ref, sc_ref, g_ref, w_ref, cos_ref, sin_ref,
                    q_ref, k_ref, v_ref, gate_ref, f_ref):
    th = x_ref.shape[1] // IN_GROUPS
    gain = g_ref[...] * (1.0 + sc_ref[0])
    for s in range(IN_GROUPS):
        rows = slice(s * th, (s + 1) * th)
        x = x_ref[0, rows, :]
        h = x * _rms_scale(x) * gain + sh_ref[0]
        proj = jnp.dot(h.astype(BF16), w_ref[...], preferred_element_type=F32)
        cosf = cos_ref[rows, :]
        sinf = sin_ref[rows, :]
        for hd in range(RET_HEADS):
            lo = hd * HEAD_DIM
            q = proj[:, lo:lo + HEAD_DIM]
            k = proj[:, RET_W + lo:RET_W + lo + HEAD_DIM] * (HEAD_DIM ** -0.5)
            q_ref[0, hd, rows, :] = (q * cosf + pltpu.roll(q, HEAD_DIM // 2, 1) * sinf).astype(BF16)
            k_ref[0, hd, rows, :] = (k * cosf + pltpu.roll(k, HEAD_DIM // 2, 1) * sinf).astype(BF16)
            v_ref[0, hd, rows, :] = proj[:, 2 * RET_W + lo:2 * RET_W + lo + HEAD_DIM].astype(BF16)
            gate_ref[0, hd, rows, :] = _silu(proj[:, 3 * RET_W + lo:3 * RET_W + lo + HEAD_DIM]).astype(BF16)
        for g in range(FOURIER_GROUPS):
            lo = 4 * RET_W + g * GROUP_DIM
            f_ref[0, g, rows, :] = proj[:, lo:lo + GROUP_DIM].astype(BF16)


def _in_proj(x, sh, sc, g, w_in, cosf, sinf, tm):
    bsz, seq, d = x.shape
    mod = pl.BlockSpec((1, 1, d), lambda b, i: (b, 0, 0))
    split = jax.ShapeDtypeStruct((bsz, RET_HEADS, seq, HEAD_DIM), BF16)
    split_spec = pl.BlockSpec((1, RET_HEADS, tm, HEAD_DIM), lambda b, i: (b, 0, i, 0))
    rope = pl.BlockSpec((tm, HEAD_DIM), lambda b, i: (i, 0))
    return pl.pallas_call(
        _in_proj_kernel,
        out_shape=(split,) * 5,
        grid=(bsz, seq // tm),
        in_specs=[pl.BlockSpec((1, tm, d), lambda b, i: (b, i, 0)), mod, mod,
                  pl.BlockSpec((1, d), lambda b, i: (0, 0)),
                  pl.BlockSpec(w_in.shape, lambda b, i: (0, 0)), rope, rope],
        out_specs=(split_spec,) * 5,
        compiler_params=_params("arbitrary", "arbitrary"),
        name="in_proj",
    )(x, sh, sc, g, w_in, cosf, sinf)


def _retention_kernel(q_ref, k_ref, v_ref, gate_ref, sf0_ref, sb0_ref, decf_ref, decb_ref,
                      o_ref, kvf_scr, kvb_scr, sf_scr, sb_scr):
    hd = pl.program_id(1)
    n_chunks = q_ref.shape[2] // CHUNK
    lgf = _log_sigmoid(decf_ref[pl.ds(hd, 1), :])
    lgb = _log_sigmoid(decb_ref[pl.ds(hd, 1), :])
    pos = lax.broadcasted_iota(jnp.int32, (CHUNK, HEAD_DIM), 0).astype(F32)
    diff = (lax.broadcasted_iota(jnp.int32, (CHUNK, CHUNK), 0)
            - lax.broadcasted_iota(jnp.int32, (CHUNK, CHUNK), 1)).astype(F32)
    lgf_c = jnp.concatenate([lgf] * (CHUNK // HEAD_DIM), axis=1)
    lgb_c = jnp.concatenate([lgb] * (CHUNK // HEAD_DIM), axis=1)
    zeta_f = jnp.exp((CHUNK - 1.0 - pos) * lgf)
    zeta_b = jnp.exp(pos * lgb)
    xi_f = jnp.exp((pos + 1.0) * lgf)
    xi_b = jnp.exp((CHUNK - pos) * lgb)
    decay = jnp.exp(jnp.where(diff >= 0, diff * lgf_c, -diff * lgb_c))
    cdec_f = jnp.exp(CHUNK * lgf)
    cdec_b = jnp.exp(CHUNK * lgb)
    tn = (((0,), (0,)), ((), ()))
    nt = (((1,), (1,)), ((), ()))

    def chunk_kv(n, carry):
        rows = pl.ds(pl.multiple_of(n * CHUNK, CHUNK), CHUNK)
        k = k_ref[0, 0, rows, :].astype(F32)
        v = v_ref[0, 0, rows, :]
        kvf_scr[n] = lax.dot_general((k * zeta_f).astype(BF16), v, tn, preferred_element_type=F32)
        kvb_scr[n] = lax.dot_general((k * zeta_b).astype(BF16), v, tn, preferred_element_type=F32)
        return carry

    lax.fori_loop(0, n_chunks, chunk_kv, 0, unroll=RET_UNROLL)

    def scan_f(n, state):
        sf_scr[n] = state.astype(BF16)
        return state * cdec_f + kvf_scr[n]

    lax.fori_loop(0, n_chunks, scan_f, sf0_ref[0, 0])

    def scan_b(i, state):
        n = n_chunks - 1 - i
        sb_scr[n] = state.astype(BF16)
        return state * cdec_b + kvb_scr[n]

    lax.fori_loop(0, n_chunks, scan_b, sb0_ref[0, 0])

    def chunk_out(n, carry):
        rows = pl.ds(pl.multiple_of(n * CHUNK, CHUNK), CHUNK)
        q = q_ref[0, 0, rows, :]
        k = k_ref[0, 0, rows, :]
        v = v_ref[0, 0, rows, :]
        scores = lax.dot_general(q, k, nt, preferred_element_type=F32) * decay
        qf = q.astype(F32)
        o = jnp.dot(scores.astype(BF16), v, preferred_element_type=F32)
        o = o + jnp.dot((qf * xi_f).astype(BF16), sf_scr[n], preferred_element_type=F32)
        o = o + jnp.dot((qf * xi_b).astype(BF16), sb_scr[n], preferred_element_type=F32)
        o = o * lax.rsqrt(jnp.mean(o * o, axis=-1, keepdims=True) + NORM_EPS)
        o_ref[0, 0, rows, :] = (o * gate_ref[0, 0, rows, :].astype(F32)).astype(BF16)
        return carry

    lax.fori_loop(0, n_chunks, chunk_out, 0, unroll=RET_UNROLL)


def _retention(q, k, v, gate, sf0, sb0, decf, decb):
    bsz, _, seq, _ = q.shape
    n_chunks = seq // CHUNK
    tok = pl.BlockSpec((1, 1, seq, HEAD_DIM), lambda b, h: (b, h, 0, 0))
    st = pl.BlockSpec((1, 1, HEAD_DIM, HEAD_DIM), lambda b, h: (b, h, 0, 0))
    dec = pl.BlockSpec((RET_HEADS, HEAD_DIM), lambda b, h: (0, 0))
    return pl.pallas_call(
        _retention_kernel,
        out_shape=jax.ShapeDtypeStruct(q.shape, BF16),
        grid=(bsz, RET_HEADS),
        in_specs=[tok, tok, tok, tok, st, st, dec, dec],
        out_specs=tok,
        scratch_shapes=[pltpu.VMEM((n_chunks, HEAD_DIM, HEAD_DIM), F32),
                        pltpu.VMEM((n_chunks, HEAD_DIM, HEAD_DIM), F32),
                        pltpu.VMEM((n_chunks, HEAD_DIM, HEAD_DIM), BF16),
                        pltpu.VMEM((n_chunks, HEAD_DIM, HEAD_DIM), BF16)],
        compiler_params=_params("arbitrary", "arbitrary"),
        name="retention",
    )(q, k, v, gate, sf0, sb0, decf, decb)


def _fourier_kernel(f_ref, w1_ref, w2_ref, wc_ref, o_ref,
                    xs, ar_s, ai_s, xr_s, xi_s):
    gh = xs.shape[0] // FFT_PITCH
    for t1 in range(gh):
        xs[pl.ds(t1 * FFT_PITCH, GRID_W), :] = f_ref[0, 0, pl.ds(t1 * GRID_W, GRID_W), :].astype(F32)
    for t2 in range(0, GRID_W, 2):
        x = jnp.concatenate([xs[pl.ds(t2 + u, gh, stride=FFT_PITCH), :] for u in range(2)], axis=1)
        r = jnp.dot(w1_ref[...], x.astype(BF16), preferred_element_type=F32)
        for u in range(2):
            lanes = slice(u * GROUP_DIM, (u + 1) * GROUP_DIM)
            ar_s[pl.ds(t2 + u, gh, stride=FFT_PITCH), :] = r[:gh, lanes]
            ai_s[pl.ds(t2 + u, gh, stride=FFT_PITCH), :] = r[gh:, lanes]
    for k1 in range(gh):
        rows = pl.ds(k1 * FFT_PITCH, GRID_W)
        stacked = jnp.concatenate([ar_s[rows, :], ai_s[rows, :]], axis=0).astype(BF16)
        r = jnp.dot(w2_ref[k1], stacked, preferred_element_type=F32)
        xr_s[pl.ds(k1, GRID_W, stride=FFT_PITCH), :] = r[:GRID_W]
        xi_s[pl.ds(k1, GRID_W, stride=FFT_PITCH), :] = r[GRID_W:]
    for kb in range(0, GRID_W, FFT_K2_BATCH):
        parts = []
        for k2 in range(kb, kb + FFT_K2_BATCH):
            rows = pl.ds(k2 * FFT_PITCH, gh)
            parts.append(jnp.concatenate([xr_s[rows, :], xi_s[rows, :]], axis=1).astype(BF16))
        res = jnp.dot(jnp.concatenate(parts, axis=0), wc_ref[...], preferred_element_type=F32)
        o_ref[0, 0, pl.ds(kb * gh, FFT_K2_BATCH * gh), :] = res.astype(BF16)


def _fourier(f, consts):
    bsz, n_groups, seq, _ = f.shape
    gh = seq // GRID_W
    tok = pl.BlockSpec((1, 1, seq, GROUP_DIM), lambda b, g: (b, g, 0, 0))
    const = lambda a: pl.BlockSpec(a.shape, lambda b, g: (0,) * a.ndim)
    names = ("w1", "w2", "wc")
    pitched = pltpu.VMEM((gh * FFT_PITCH, GROUP_DIM), F32)
    return pl.pallas_call(
        _fourier_kernel,
        out_shape=jax.ShapeDtypeStruct(f.shape, BF16),
        grid=(bsz, n_groups),
        in_specs=[tok] + [const(consts[n]) for n in names],
        out_specs=tok,
        scratch_shapes=[pitched] * 5,
        compiler_params=_params("arbitrary", "arbitrary"),
        name="fourier",
    )(f, *[consts[n] for n in names])


def _dft_constants(seq):
    gh = seq // GRID_W
    def cs(n_rows, n_cols, period):
        ang = 2.0 * np.pi * np.outer(np.arange(n_rows), np.arange(n_cols)) / period
        return np.cos(ang), np.sin(ang)
    c1, s1 = cs(gh, gh, gh)
    w1 = np.concatenate([c1, -s1], axis=0) / gh
    c2, s2 = cs(GRID_W, GRID_W, GRID_W)
    cc, sc = cs(GROUP_DIM, GROUP_DIM, GROUP_DIM)
    wc = np.concatenate([cc, sc], axis=0) / math.sqrt(GROUP_DIM)
    twc, tws = cs(gh, GRID_W, seq)
    p = c2[None] * twc[:, None, :] - s2[None] * tws[:, None, :]
    q = s2[None] * twc[:, None, :] + c2[None] * tws[:, None, :]
    w2 = np.concatenate([np.concatenate([p, q], axis=2), np.concatenate([-q, p], axis=2)], axis=1)
    w2 = w2 / math.sqrt(GRID_W * gh) * gh
    t = np.arange(seq)
    n_freq = HEAD_DIM // 4
    freqs = ROPE_BASE ** (-np.arange(n_freq, dtype=np.float32) / n_freq)
    rang = np.concatenate([(t // GRID_W).astype(np.float32)[:, None] * freqs,
                           (t % GRID_W).astype(np.float32)[:, None] * freqs], axis=-1).astype(np.float32)
    cosf = np.concatenate([np.cos(rang), np.cos(rang)], axis=-1)
    sinf = np.concatenate([-np.sin(rang), np.sin(rang)], axis=-1)
    as_f32 = lambda a: jnp.asarray(np.ascontiguousarray(a), F32)
    as_bf16 = lambda a: jnp.asarray(np.ascontiguousarray(a), F32).astype(BF16)
    return dict(w1=as_bf16(w1), w2=as_bf16(w2), wc=as_bf16(wc),
                cosf=as_f32(cosf), sinf=as_f32(sinf))


def _out_ffn_kernel(x_ref, ret_ref, four_ref, mod_ref, gpost_ref, gpre_ref, gfpost_ref,
                    wo_ref, wup_ref, taps_ref, wd_ref, o_ref, h_ref, u_ref, act_ref):
    tm = x_ref.shape[1]
    th = tm // FF_GROUPS
    d_ff = wd_ref.shape[0]
    gt_a, sh_f, sc_f, gt_f = mod_ref[0, 0:1], mod_ref[0, 1:2], mod_ref[0, 2:3], mod_ref[0, 3:4]
    groups = [slice(s * th, (s + 1) * th) for s in range(FF_GROUPS)]
    gain_a = gpost_ref[...] * gt_a
    gain_f = gpre_ref[...] * (1.0 + sc_f)
    gain_o = gfpost_ref[...] * gt_f

    for rows in groups:
        mixed = [ret_ref[0, i, rows, :] for i in range(RET_HEADS)]
        mixed += [four_ref[0, i, rows, :] for i in range(FOURIER_GROUPS)]
        mix = jnp.dot(jnp.concatenate(mixed, axis=1), wo_ref[...], preferred_element_type=F32)
        x1 = x_ref[0, rows, :] + mix * _rms_scale(mix) * gain_a
        o_ref[0, rows, :] = x1
        h_ref[rows, :] = (x1 * _rms_scale(x1) * gain_f + sh_f).astype(BF16)
    u_ref[:, :, 0:U_PAD, :] = jnp.zeros(u_ref.shape[:2] + (U_PAD, LANES), F32)
    u_ref[:, :, U_PAD + tm:, :] = jnp.zeros(u_ref.shape[:2] + (U_PAD, LANES), F32)

    lane_tiles = FF_CHUNK // LANES
    sub = lax.broadcasted_iota(jnp.int32, (SUBLANES, LANES), 0)
    not_first = (sub != 0).astype(F32)
    not_last = (sub != SUBLANES - 1).astype(F32)

    def conv(slot, lt, lo):
        w_prev, w_cur, w_next, bias = (taps_ref[i:i + 1, lo:lo + LANES] for i in range(4))
        blocks = []
        for r in range(0, tm, GRID_W):
            prev = u_ref[slot, lt, U_PAD - 1 + r:U_PAD - 1 + r + GRID_W, :]
            nxt = u_ref[slot, lt, U_PAD + 1 + r:U_PAD + 1 + r + GRID_W, :]
            cur = u_ref[slot, lt, U_PAD + r:U_PAD + r + GRID_W, :]
            prev = jnp.concatenate([prev[:SUBLANES] * not_first, prev[SUBLANES:]], axis=0)
            nxt = jnp.concatenate([nxt[:GRID_W - SUBLANES], nxt[GRID_W - SUBLANES:] * not_last], axis=0)
            blocks.append(prev * w_prev + nxt * w_next + cur * w_cur + bias)
        return jnp.concatenate(blocks, axis=0)

    for j in range(d_ff // FF_CHUNK):
        cols = (j * FF_CHUNK, d_ff + j * FF_CHUNK)
        slots = (2 * j, 2 * j + 1)
        h = h_ref[...]
        for slot, lo in zip(slots, cols):
            u = jnp.dot(h, wup_ref[:, lo:lo + FF_CHUNK], preferred_element_type=F32)
            for lt in range(lane_tiles):
                u_ref[slot, lt, U_PAD:U_PAD + tm, :] = u[:, lt * LANES:(lt + 1) * LANES]
        for lt in range(lane_tiles):
            a = conv(slots[0], lt, cols[0] + lt * LANES)
            b = conv(slots[1], lt, cols[1] + lt * LANES)
            lo = j * FF_CHUNK + lt * LANES
            act_ref[:, lo:lo + LANES] = (_silu(a) * b).astype(BF16)

    for rows in groups:
        ffn = jnp.dot(act_ref[rows, :], wd_ref[...], preferred_element_type=F32)
        o_ref[0, rows, :] = o_ref[0, rows, :] + ffn * _rms_scale(ffn) * gain_o


def _out_ffn(x, ret, four, mod, gpost, gpre, gfpost, wo, wup, taps, wd, tm):
    bsz, seq, d = x.shape
    d_ff = wd.shape[0]
    tok = lambda w: pl.BlockSpec((1, tm, w), lambda b, i: (b, i, 0))
    split = lambda n: pl.BlockSpec((1, n, tm, LANES), lambda b, i: (b, 0, i, 0))
    vec = pl.BlockSpec((1, d), lambda b, i: (0, 0))
    const = lambda a: pl.BlockSpec(a.shape, lambda b, i: (0,) * a.ndim, pipeline_mode=pl.Buffered(1))
    return pl.pallas_call(
        _out_ffn_kernel,
        out_shape=jax.ShapeDtypeStruct((bsz, seq, d), F32),
        grid=(bsz, seq // tm),
        in_specs=[tok(d), split(RET_HEADS), split(FOURIER_GROUPS),
                  pl.BlockSpec((1, 4, d), lambda b, i: (b, 0, 0)), vec, vec, vec,
                  const(wo), const(wup), const(taps), const(wd)],
        out_specs=tok(d),
        scratch_shapes=[pltpu.VMEM((tm, d), BF16),
                        pltpu.VMEM((2 * (d_ff // FF_CHUNK), FF_CHUNK // LANES, tm + 2 * U_PAD, LANES), F32),
                        pltpu.VMEM((tm, d_ff), BF16)],
        compiler_params=_params("arbitrary", "arbitrary"),
        name="out_ffn",
    )(x, ret, four, mod, gpost, gpre, gfpost, wo, wup, taps, wd)


def _layer(x, ctx, cond, consts, w_ada, b_ada, g_mix_pre, g_mix_post, g_ffn_pre, g_ffn_post,
           w_in, dec_f, dec_b, w_out, w_up, conv_w, conv_b, w_down):
    bsz, seq, d = x.shape
    d_ff = w_down.shape[0]
    row = lambda v: v.reshape(1, -1)

    mod = _adaln(cond, w_ada, row(b_ada))
    mod = mod.reshape(COND_ROWS, N_MOD, d)
    lat = mod[:bsz]
    cmod = mod[bsz]

    decf = jnp.broadcast_to(dec_f[:, None], (RET_HEADS, HEAD_DIM))
    decb = jnp.broadcast_to(dec_b[:, None], (RET_HEADS, HEAD_DIM))
    w_in_bf = w_in.astype(BF16)
    sf0, sb0 = _ctx_states(ctx, cmod[0:1], cmod[1:2], row(g_mix_pre), w_in_bf, decf, decb)

    q, k, v, gate, f = _in_proj(x, lat[:, 0:1], lat[:, 1:2], row(g_mix_pre), w_in_bf,
                                consts["cosf"], consts["sinf"], tm=1024)
    ret = _retention(q, k, v, gate, sf0, sb0, decf, decb)
    four = _fourier(f, consts)

    taps = jnp.concatenate([conv_w, conv_b[None], jnp.zeros((SUBLANES - CONV_WIDTH - 1, 2 * d_ff), F32)], axis=0)
    return _out_ffn(x, ret, four, lat[:, 2:6], row(g_mix_post), row(g_ffn_pre), row(g_ffn_post),
                    w_out.astype(BF16), w_up.astype(BF16), taps, w_down.astype(BF16), tm=512)


def kernel(x, c, ctx, c_ctx, w_ada, b_ada, g_mix_pre, g_mix_post, g_ffn_pre, g_ffn_post, w_in,
           ret_decay_fwd, ret_decay_bwd, w_out, w_up, conv_w, conv_b, w_down):
    bsz, seq, d = x.shape
    depth = w_ada.shape[0]
    assert depth == 1, "the context stream update of deeper stacks is not implemented"
    assert bsz + 1 <= COND_ROWS and seq % (GRID_W * GRID_W) == 0 and seq // GRID_W == GRID_W
    assert bsz % CTX_BATCH == 0
    consts = _dft_constants(seq)
    cond = jnp.concatenate([c, c_ctx[None], jnp.zeros((COND_ROWS - bsz - 1, d), F32)], axis=0)
    layer = 0
    return _layer(x, ctx, cond, consts, w_ada[layer], b_ada[layer], g_mix_pre[layer], g_mix_post[layer],
                  g_ffn_pre[layer], g_ffn_post[layer], w_in[layer], ret_decay_fwd[layer],
                  ret_decay_bwd[layer], w_out[layer], w_up[layer], conv_w[layer], conv_b[layer],
                  w_down[layer])
```
